```python
import math
import jax, jax.numpy as jnp
from jax import lax
import numpy as np

D_MODEL = 2048
BATCH = 4
SEQ = 4096
DEPTH = 2

CHUNK = 64
Q_BLOCK = 128
N_BRANCH = 4
HEAD_DIM = 128
HEADS = D_MODEL // (N_BRANCH * HEAD_DIM)
BRANCH_W = HEADS * HEAD_DIM
A_LEFT_CHUNKS = 8
A_BAND = (A_LEFT_CHUNKS + 1) * CHUNK
REL_CLIP = 128
DIFF_DIM = HEAD_DIM // 2
IDX_HEADS = 8
IDX_DIM = 64
TOPK_MAX = 256
ROPE_THETA = 10000.0
D_FF = 4 * D_MODEL
EPS = 1e-6

SEGMENTS = (
    ("a_q", BRANCH_W), ("a_k", BRANCH_W), ("a_v", BRANCH_W),
    ("b_q", BRANCH_W), ("b_k", BRANCH_W), ("b_v", BRANCH_W),
    ("c_q", BRANCH_W), ("c_k", BRANCH_W), ("c_v", BRANCH_W), ("c_f", HEADS),
    ("d_q", BRANCH_W), ("d_k", HEAD_DIM), ("d_v", HEAD_DIM),
    ("d_iq", IDX_HEADS * IDX_DIM), ("d_ik", IDX_DIM), ("d_iw", IDX_HEADS),
    ("gate", N_BRANCH * D_MODEL),
)
N_IN = (10 * BRANCH_W + HEADS + 2 * HEAD_DIM + IDX_HEADS * IDX_DIM + IDX_DIM
        + IDX_HEADS + N_BRANCH * D_MODEL)

kernel_name = "hybrid_gated_chunk_causal_encoder"


def _split(z):
    out = {}
    off = 0
    for name, width in SEGMENTS:
        out[name] = z[..., off:off + width]
        off += width
    return out


def rmsnorm(x, g):
    xf = x.astype(jnp.float32)
    y = xf * lax.rsqrt(jnp.mean(xf * xf, axis=-1, keepdims=True) + EPS)
    return (y * g.astype(jnp.float32)).astype(x.dtype)


def rope_tables(seq, dim):
    inv = ROPE_THETA ** (-jnp.arange(0, dim, 2, dtype=jnp.float32) / dim)
    ang = jnp.arange(seq, dtype=jnp.float32)[:, None] * inv[None, :]
    return jnp.cos(ang), jnp.sin(ang)


def apply_rope(x, cos, sin):
    half = x.shape[-1] // 2
    shape = (x.shape[1],) + (1,) * (x.ndim - 3) + (half,)
    c = cos.reshape(shape).astype(x.dtype)
    s = sin.reshape(shape).astype(x.dtype)
    x1, x2 = x[..., :half], x[..., half:]
    return jnp.concatenate([x1 * c - x2 * s, x1 * s + x2 * c], axis=-1)


def chunk_causal_mask(start, end):
    qpos = jnp.arange(start, end)
    kpos = jnp.arange(end)
    return (kpos[None, :] // CHUNK) <= (qpos[:, None] // CHUNK)


def chunk_band_attention(q, k, v, rel_bias):
    B, S, H, d = q.shape
    nc = S // CHUNK
    qc = q.reshape(B, nc, CHUNK, H, d)
    pad = ((0, 0), (A_LEFT_CHUNKS * CHUNK, 0), (0, 0), (0, 0))
    kp = jnp.pad(k, pad).reshape(B, nc + A_LEFT_CHUNKS, CHUNK, H, d)
    vp = jnp.pad(v, pad).reshape(B, nc + A_LEFT_CHUNKS, CHUNK, H, d)
    band = jnp.arange(nc)[:, None] + jnp.arange(A_LEFT_CHUNKS + 1)[None, :]
    kb = kp[:, band].reshape(B, nc, A_BAND, H, d)
    vb = vp[:, band].reshape(B, nc, A_BAND, H, d)
    s = jnp.einsum('bcqhd,bckhd->bhcqk', qc, kb).astype(jnp.float32) * (d ** -0.5)
    rel = A_LEFT_CHUNKS * CHUNK + jnp.arange(CHUNK)[:, None] - jnp.arange(A_BAND)[None, :]
    bias = rel_bias.astype(jnp.float32)[:, jnp.clip(rel, -REL_CLIP, REL_CLIP) + REL_CLIP]
    valid = jnp.repeat(band >= A_LEFT_CHUNKS, CHUNK, axis=1)
    s = jnp.where(valid[None, None, :, None, :], s + bias[None, :, None], -jnp.inf)
    p = jax.nn.softmax(s, axis=-1)
    o = jnp.einsum('bhcqk,bckhd->bcqhd', p.astype(v.dtype), vb)
    return o.reshape(B, S, H, d)


def diff_attention(q, k, v, lam, lambda_init, sub_norm_g):
    B, S, H, _, dd = q.shape
    scale = dd ** -0.5
    outs = []
    for start in range(0, S, Q_BLOCK):
        end = start + Q_BLOCK
        s = jnp.einsum('bqhnd,bkhnd->bnhqk', q[:, start:end], k[:, :end]).astype(jnp.float32) * scale
        s = jnp.where(chunk_causal_mask(start, end), s, -jnp.inf)
        p = jax.nn.softmax(s, axis=-1)
        a = p[:, 0] - lam * p[:, 1]
        outs.append(jnp.einsum('bhqk,bkhd->bqhd', a.astype(v.dtype), v[:, :end]))
    o = jnp.concatenate(outs, axis=1)
    return rmsnorm(o, sub_norm_g) * (1.0 - lambda_init)


def forgetting_attention(q, k, v, log_f):
    B, S, H, d = q.shape
    scale = d ** -0.5
    cum = jnp.transpose(lax.cumsum(log_f, axis=1), (0, 2, 1))
    outs = []
    for start in range(0, S, Q_BLOCK):
        end = start + Q_BLOCK
        s = jnp.einsum('bqhd,bkhd->bhqk', q[:, start:end], k[:, :end]).astype(jnp.float32) * scale
        s = s + (cum[:, :, start:end, None] - cum[:, :, None, :end])
        causal = jnp.arange(end)[None, :] <= jnp.arange(start, end)[:, None]
        p = jax.nn.softmax(jnp.where(causal, s, -jnp.inf), axis=-1)
        outs.append(jnp.einsum('bhqk,bkhd->bqhd', p.astype(v.dtype), v[:, :end]))
    return jnp.concatenate(outs, axis=1)


def _gather_rows(src, idx):
    return jax.vmap(lambda s_, i_: s_[i_])(src, idx)


def indexed_sparse_attention(q, k, v, iq, ik, iw, k_sel):
    B, S, H, d = q.shape
    scale = d ** -0.5
    outs = []
    for start in range(0, S, Q_BLOCK):
        end = start + Q_BLOCK
        dots = jnp.einsum('bqhd,bkd->bqhk', iq[:, start:end], ik[:, :end]).astype(jnp.float32)
        score = jnp.einsum('bqh,bqhk->bqk', iw[:, start:end].astype(jnp.float32), jax.nn.relu(dots))
        score = jnp.where(chunk_causal_mask(start, end)[None], score, -jnp.inf)
        _, sel = lax.top_k(score, min(k_sel, end))
        kg = _gather_rows(k[:, :end], sel)
        vg = _gather_rows(v[:, :end], sel)
        s = jnp.einsum('bqhd,bqkd->bhqk', q[:, start:end], kg).astype(jnp.float32) * scale
        valid = (sel // CHUNK) <= (jnp.arange(start, end) // CHUNK)[None, :, None]
        p = jax.nn.softmax(jnp.where(valid[:, None], s, -jnp.inf), axis=-1)
        outs.append(jnp.einsum('bhqk,bqkd->bqhd', p.astype(v.dtype), vg))
    return jnp.concatenate(outs, axis=1)


def setup_inputs(seed: int = 0) -> dict:
    key = jax.random.key(seed)
    ks = jax.random.split(key, 20)

    def nrm(k, shape, scale):
        return jax.random.normal(k, shape, jnp.float32) * scale

    return {
        "x": nrm(ks[0], (BATCH, SEQ, D_MODEL), 1.0),
        "norm1_g": 1.0 + nrm(ks[1], (DEPTH, D_MODEL), 0.02),
        "norm2_g": 1.0 + nrm(ks[2], (DEPTH, D_MODEL), 0.02),
        "final_g": 1.0 + nrm(ks[3], (D_MODEL,), 0.02),
        "w_in": nrm(ks[4], (DEPTH, D_MODEL, N_IN), D_MODEL ** -0.5),
        "b_gate": nrm(ks[5], (DEPTH, N_BRANCH * D_MODEL), 0.02),
        "b_forget": 2.0 + nrm(ks[6], (DEPTH, HEADS), 0.5),
        "rel_bias": nrm(ks[7], (DEPTH, HEADS, 2 * REL_CLIP + 1), 0.5),
        "lambda_q1": nrm(ks[8], (DEPTH, DIFF_DIM), 0.1),
        "lambda_k1": nrm(ks[9], (DEPTH, DIFF_DIM), 0.1),
        "lambda_q2": nrm(ks[10], (DEPTH, DIFF_DIM), 0.1),
        "lambda_k2": nrm(ks[11], (DEPTH, DIFF_DIM), 0.1),
        "diff_norm_g": 1.0 + nrm(ks[12], (DEPTH, HEAD_DIM), 0.02),
        "w_branch": nrm(ks[13], (DEPTH, N_BRANCH, BRANCH_W, D_MODEL), BRANCH_W ** -0.5),
        "w_out": nrm(ks[14], (DEPTH, D_MODEL, D_MODEL), D_MODEL ** -0.5),
        "w_ff1": nrm(ks[15], (DEPTH, D_MODEL, D_FF), D_MODEL ** -0.5),
        "w_ff2": nrm(ks[16], (DEPTH, D_FF, D_MODEL), D_FF ** -0.5),
    }


def reference(x, norm1_g, norm2_g, final_g, w_in, b_gate, b_forget, rel_bias,
              lambda_q1, lambda_k1, lambda_q2, lambda_k2, diff_norm_g,
              w_branch, w_out, w_ff1, w_ff2):
    B, S, _ = x.shape
    k_sel = min(TOPK_MAX, S // 4)
    cos128, sin128 = rope_tables(S, HEAD_DIM)
    cos64, sin64 = rope_tables(S, DIFF_DIM)
    for l in range(DEPTH):
        h = rmsnorm(x, norm1_g[l])
        p = _split(h @ w_in[l])

        oa = chunk_band_attention(p["a_q"].reshape(B, S, HEADS, HEAD_DIM),
                                  p["a_k"].reshape(B, S, HEADS, HEAD_DIM),
                                  p["a_v"].reshape(B, S, HEADS, HEAD_DIM), rel_bias[l])

        lambda_init = 0.8 - 0.6 * math.exp(-0.3 * l)
        lam = (jnp.exp(jnp.sum(lambda_q1[l].astype(jnp.float32) * lambda_k1[l].astype(jnp.float32)))
               - jnp.exp(jnp.sum(lambda_q2[l].astype(jnp.float32) * lambda_k2[l].astype(jnp.float32)))
               + lambda_init)
        qb = apply_rope(p["b_q"].reshape(B, S, HEADS, 2, DIFF_DIM), cos64, sin64)
        kb = apply_rope(p["b_k"].reshape(B, S, HEADS, 2, DIFF_DIM), cos64, sin64)
        ob = diff_attention(qb, kb, p["b_v"].reshape(B, S, HEADS, HEAD_DIM), lam, lambda_init, diff_norm_g[l])

        log_f = jax.nn.log_sigmoid(p["c_f"].astype(jnp.float32) + b_forget[l].astype(jnp.float32))
        oc = forgetting_attention(p["c_q"].reshape(B, S, HEADS, HEAD_DIM),
                                  p["c_k"].reshape(B, S, HEADS, HEAD_DIM),
                                  p["c_v"].reshape(B, S, HEADS, HEAD_DIM), log_f)

        qd = apply_rope(p["d_q"].reshape(B, S, HEADS, HEAD_DIM), cos128, sin128)
        kd = apply_rope(p["d_k"], cos128, sin128)
        iq = apply_rope(p["d_iq"].reshape(B, S, IDX_HEADS, IDX_DIM), cos64, sin64) * (IDX_DIM ** -0.5)
        ik = apply_rope(p["d_ik"], cos64, sin64)
        iw = p["d_iw"] * (IDX_HEADS ** -0.5)
        od = indexed_sparse_attention(qd, kd, p["d_v"], iq, ik, iw, k_sel)

        gates = jax.nn.sigmoid(p["gate"] + b_gate[l]).reshape(B, S, N_BRANCH, D_MODEL)
        merged = None
        for i, o in enumerate((oa, ob, oc, od)):
            term = gates[:, :, i] * (o.reshape(B, S, BRANCH_W) @ w_branch[l, i])
            merged = term if merged is None else merged + term
        x = x + merged @ w_out[l]

        h2 = rmsnorm(x, norm2_g[l])
        x = x + jnp.square(jax.nn.relu(h2 @ w_ff1[l])) @ w_ff2[l]
    return rmsnorm(x, final_g)
```

```python
import functools
import math

import numpy as np
import jax
import jax.numpy as jnp
from jax import lax
from jax.experimental import pallas as pl
from jax.experimental.pallas import tpu as pltpu

F32 = jnp.float32
BF16 = jnp.bfloat16
I32 = jnp.int32

CHUNK = 64
N_BRANCH = 4
HEAD_DIM = 128
HEADS = 4
BRANCH_W = HEADS * HEAD_DIM
A_LEFT_CHUNKS = 8
A_LEFT = A_LEFT_CHUNKS * CHUNK
REL_CLIP = 128
DIFF_DIM = HEAD_DIM // 2
IDX_HEADS = 8
IDX_DIM = 64
TOPK_MAX = 256
ROPE_THETA = 10000.0
EPS = 1e-6

LANES = 128
NEG = -1e30
INT_MIN = -2147483648
VMEM_LIMIT = 56 * 1024 * 1024

TN = 512
T_AQ, T_AK, T_AV, T_BQ, T_BK, T_BV, T_CQ, T_CK, T_CV, T_DQ, T_DIQ, T_MISC, T_GATE = range(13)
N_TILES = T_GATE + N_BRANCH * 4
MISC_DK, MISC_DV, MISC_IK, MISC_SIDE = 0, 128, 256, 384
SIDE_F, SIDE_IW = 0, 4


def _dot(a, b):
    return jnp.dot(a, b, preferred_element_type=F32)


def _dot_t(a, b):
    return lax.dot_general(a, b, (((1,), (1,)), ((), ())), preferred_element_type=F32)


def _params(*sem):
    return pltpu.CompilerParams(dimension_semantics=sem, vmem_limit_bytes=VMEM_LIMIT)


def _rope_half(blk, cos, sin_signed, half):
    if 2 * half == LANES:
        rot = pltpu.roll(blk, half, axis=1)
    else:
        lane = lax.broadcasted_iota(I32, blk.shape, 1)
        first = (lane & (2 * half - 1)) < half
        rot = jnp.where(first, pltpu.roll(blk, LANES - half, axis=1), pltpu.roll(blk, half, axis=1))
    return blk * cos + rot * sin_signed


def _in_proj_kernel(x_ref, g_ref, w_ref, bias_ref, rope_ref, bf_ref, z_ref, side_ref, h_ref):
    j = pl.program_id(1)

    @pl.when(j == 0)
    def _():
        x = x_ref[...]
        y = x * lax.rsqrt(jnp.mean(x * x, axis=-1, keepdims=True) + EPS)
        h_ref[...] = (y * g_ref[...]).astype(BF16)

    acc = _dot(h_ref[...], w_ref[...])

    def cols(a, c):
        return a[:, c * LANES:(c + 1) * LANES]

    def rope64(blk):
        return _rope_half(blk, rope_ref[:, 0:128], rope_ref[:, 128:256], 32)

    def rope128(blk):
        return _rope_half(blk, rope_ref[:, 256:384], rope_ref[:, 384:512], 64)

    is_plain = ((j <= T_AV) | (j == T_BV) | ((j >= T_CQ) & (j <= T_CV)))

    @pl.when(is_plain)
    def _():
        z_ref[...] = acc.astype(BF16)

    @pl.when((j == T_BQ) | (j == T_BK))
    def _():
        for c in range(4):
            z_ref[:, c * LANES:(c + 1) * LANES] = rope64(cols(acc, c)).astype(BF16)

    @pl.when(j == T_DQ)
    def _():
        for c in range(4):
            z_ref[:, c * LANES:(c + 1) * LANES] = rope128(cols(acc, c)).astype(BF16)

    @pl.when(j == T_DIQ)
    def _():
        for c in range(4):
            z_ref[:, c * LANES:(c + 1) * LANES] = (rope64(cols(acc, c)) * (IDX_DIM ** -0.5)).astype(BF16)

    @pl.when(j == T_MISC)
    def _():
        z_ref[:, MISC_DK:MISC_DK + LANES] = rope128(cols(acc, 0)).astype(BF16)
        z_ref[:, MISC_DV:MISC_DV + LANES] = cols(acc, 1).astype(BF16)
        z_ref[:, MISC_IK:MISC_IK + LANES] = rope64(cols(acc, 2)).astype(BF16)
        raw = cols(acc, 3)
        z_ref[:, MISC_SIDE:MISC_SIDE + LANES] = raw.astype(BF16)
        lane = lax.broadcasted_iota(I32, raw.shape, 1)
        xf = raw + bf_ref[...]
        log_f = jnp.minimum(xf, 0.0) - jnp.log(1.0 + jnp.exp(-jnp.abs(xf)))
        iw = raw * (IDX_HEADS ** -0.5)
        side_ref[...] = jnp.where(lane < SIDE_IW, log_f,
                                  jnp.where(lane < SIDE_IW + IDX_HEADS, iw, 0.0))

    @pl.when(j >= T_GATE)
    def _():
        z_ref[...] = (1.0 / (1.0 + jnp.exp(-(acc + bias_ref[...])))).astype(BF16)


def _in_proj(x2, g, w, bias, rope, bf, seq, tm):
    T, D = x2.shape
    n_pos = seq // tm
    return pl.pallas_call(
        _in_proj_kernel,
        grid=(T // tm, N_TILES),
        in_specs=[
            pl.BlockSpec((tm, D), lambda i, j: (i, 0)),
            pl.BlockSpec((1, D), lambda i, j: (0, 0)),
            pl.BlockSpec((D, TN), lambda i, j: (0, j)),
            pl.BlockSpec((1, TN), lambda i, j: (0, j)),
            pl.BlockSpec((tm, TN), lambda i, j: (i % n_pos, 0)),
            pl.BlockSpec((1, LANES), lambda i, j: (0, 0)),
        ],
        out_specs=[
            pl.BlockSpec((tm, TN), lambda i, j: (i, j)),
            pl.BlockSpec((tm, LANES), lambda i, j: (i, 0)),
        ],
        out_shape=[
            jax.ShapeDtypeStruct((T, N_TILES * TN), BF16),
            jax.ShapeDtypeStruct((T, LANES), F32),
        ],
        scratch_shapes=[pltpu.VMEM((tm, D), BF16)],
        compiler_params=_params("arbitrary", "arbitrary"),
        name="in_proj",
    )(x2, g, w, bias, rope, bf)


def _cum_kernel(side_ref, col_ref, row_ref, *, blk):
    S = side_ref.shape[0]
    r = lax.broadcasted_iota(I32, (blk, blk), 0)
    c = lax.broadcasted_iota(I32, (blk, blk), 1)
    tri = jnp.where(c <= r, 1.0, 0.0).astype(BF16)
    carry = jnp.zeros((1, LANES), F32)
    for b in range(S // blk):
        xb = side_ref[b * blk:(b + 1) * blk, :]
        hi = xb.astype(BF16)
        r1 = xb - hi.astype(F32)
        mid = r1.astype(BF16)
        lo = (r1 - mid.astype(F32)).astype(BF16)
        cb = (_dot(tri, hi) + _dot(tri, mid)) + _dot(tri, lo) + carry
        col_ref[b * blk:(b + 1) * blk, :] = cb
        row_ref[b] = cb.T[0:8, :]
        carry = cb[blk - 1:blk, :]


def _cum(side3, blk):
    B, S, _ = side3.shape
    return pl.pallas_call(
        functools.partial(_cum_kernel, blk=blk),
        grid=(B,),
        in_specs=[pl.BlockSpec((None, S, LANES), lambda b: (b, 0, 0))],
        out_specs=[
            pl.BlockSpec((None, S, LANES), lambda b: (b, 0, 0)),
            pl.BlockSpec((None, S // blk, 8, blk), lambda b: (b, 0, 0, 0)),
        ],
        out_shape=[
            jax.ShapeDtypeStruct((B, S, LANES), F32),
            jax.ShapeDtypeStruct((B, S // blk, 8, blk), F32),
        ],
        compiler_params=_params("arbitrary"),
        name="forget_cumsum",
    )(side3)


def _online(s, v, m, l, acc):
    m_new = jnp.maximum(m, jnp.max(s, axis=-1, keepdims=True))
    alpha = jnp.exp(m - m_new)
    p = jnp.exp(s - m_new)
    l = alpha * l + jnp.sum(p, axis=-1, keepdims=True)
    acc = alpha * acc + _dot(p.astype(BF16), v)
    return m_new, l, acc


def _softmax_init(rows):
    return (jnp.full((rows, 1), NEG, F32), jnp.zeros((rows, 1), F32), jnp.zeros((rows, HEAD_DIM), F32))


def _head(h):
    return slice(h * HEAD_DIM, (h + 1) * HEAD_DIM)


def _attn_a_kernel(q_ref, k_ref, v_ref, bias_ref, o_ref, *, tq, win):
    i = pl.program_id(1)
    start = pl.multiple_of(jnp.maximum(i * tq - A_LEFT, 0), tq)
    scale = HEAD_DIM ** -0.5
    for h in range(HEADS):
        q = q_ref[:, _head(h)]
        k = k_ref[pl.ds(start, win), _head(h)]
        v = v_ref[pl.ds(start, win), _head(h)]
        s = _dot_t(q, k) * scale + bias_ref[h]
        m = jnp.max(s, axis=-1, keepdims=True)
        p = jnp.exp(s - m)
        l = jnp.sum(p, axis=-1, keepdims=True)
        o_ref[:, _head(h)] = (_dot(p.astype(BF16), v) / l).astype(BF16)


def _attn_a(z3, table, tq):
    B, S, _ = z3.shape
    nvar, _, _, win = table.shape
    return pl.pallas_call(
        functools.partial(_attn_a_kernel, tq=tq, win=win),
        grid=(B, S // tq),
        in_specs=[
            pl.BlockSpec((None, tq, TN), lambda b, i: (b, i, T_AQ)),
            pl.BlockSpec((None, S, TN), lambda b, i: (b, 0, T_AK)),
            pl.BlockSpec((None, S, TN), lambda b, i: (b, 0, T_AV)),
            pl.BlockSpec((None, HEADS, tq, win), lambda b, i: (jnp.minimum(i, nvar - 1), 0, 0, 0)),
        ],
        out_specs=pl.BlockSpec((None, tq, BRANCH_W), lambda b, i: (b, i, 0)),
        out_shape=jax.ShapeDtypeStruct((B, S, BRANCH_W), BF16),
        compiler_params=_params("arbitrary", "arbitrary"),
        name="attn_band",
    )(z3, z3, z3, table)


def _band_table(rel_bias_l, tq):
    win = A_LEFT + tq
    nvar = A_LEFT // tq + 1
    qi = np.arange(tq)[:, None]
    kj = np.arange(win)[None, :]
    idx, valid = [], []
    for v in range(nvar):
        kk = kj + (A_LEFT - v * tq)
        rel = A_LEFT + qi - kk
        cq = (A_LEFT + qi) // CHUNK
        ck = kk // CHUNK
        valid.append((ck <= cq) & (ck >= cq - A_LEFT_CHUNKS))
        idx.append(np.clip(rel, -REL_CLIP, REL_CLIP) + REL_CLIP)
    idx = np.stack(idx)
    valid = np.stack(valid)
    tbl = rel_bias_l.astype(F32)[:, idx]
    tbl = jnp.where(valid[None], tbl, NEG)
    return jnp.transpose(tbl, (1, 0, 2, 3))


def _attn_b_kernel(q_ref, k_ref, v_ref, lam_ref, g_ref, o_ref, *, tq, lambda_init):
    i = pl.program_id(1)
    scale = DIFF_DIM ** -0.5
    lv = lam_ref[...]
    lam = (jnp.exp(jnp.sum(lv[0:1] * lv[1:2], axis=-1, keepdims=True))
           - jnp.exp(jnp.sum(lv[2:3] * lv[3:4], axis=-1, keepdims=True)) + lambda_init)
    lane = lax.broadcasted_iota(I32, (tq, HEAD_DIM), 1)
    qchunk = (i * tq + lax.broadcasted_iota(I32, (tq, tq), 0)) >> 6
    kchunk = (i * tq + lax.broadcasted_iota(I32, (tq, tq), 1)) >> 6
    ok = kchunk <= qchunk
    for h in range(HEADS):
        q = q_ref[:, _head(h)]
        q1 = jnp.where(lane < DIFF_DIM, q, jnp.zeros_like(q))
        q2 = jnp.where(lane >= DIFF_DIM, q, jnp.zeros_like(q))

        def step(t, carry, masked, q1=q1, q2=q2, h=h):
            ks = pl.multiple_of(t * tq, tq)
            k = k_ref[pl.ds(ks, tq), _head(h)]
            v = v_ref[pl.ds(ks, tq), _head(h)]
            s1 = _dot_t(q1, k) * scale
            s2 = _dot_t(q2, k) * scale
            if masked:
                s1 = jnp.where(ok, s1, NEG)
                s2 = jnp.where(ok, s2, NEG)
            return _online(s1, v, *carry[0:3]) + _online(s2, v, *carry[3:6])

        init = _softmax_init(tq) + _softmax_init(tq)
        carry = lax.fori_loop(0, i, functools.partial(step, masked=False), init)
        m1, l1, a1, m2, l2, a2 = step(i, carry, True)
        o = a1 / l1 - lam * (a2 / l2)
        y = o * lax.rsqrt(jnp.mean(o * o, axis=-1, keepdims=True) + EPS)
        o_ref[:, _head(h)] = ((y * g_ref[...]) * (1.0 - lambda_init)).astype(BF16)


def _attn_b(z3, lamv, g, tq, lambda_init):
    B, S, _ = z3.shape
    return pl.pallas_call(
        functools.partial(_attn_b_kernel, tq=tq, lambda_init=lambda_init),
        grid=(B, S // tq),
        in_specs=[
            pl.BlockSpec((None, tq, TN), lambda b, i: (b, i, T_BQ)),
            pl.BlockSpec((None, S, TN), lambda b, i: (b, 0, T_BK)),
            pl.BlockSpec((None, S, TN), lambda b, i: (b, 0, T_BV)),
            pl.BlockSpec((4, DIFF_DIM), lambda b, i: (0, 0)),
            pl.BlockSpec((1, HEAD_DIM), lambda b, i: (0, 0)),
        ],
        out_specs=pl.BlockSpec((None, tq, BRANCH_W), lambda b, i: (b, i, 0)),
        out_shape=jax.ShapeDtypeStruct((B, S, BRANCH_W), BF16),
        compiler_params=_params("arbitrary", "arbitrary"),
        name="attn_diff",
    )(z3, z3, z3, lamv, g)


def _attn_c_kernel(q_ref, k_ref, v_ref, ccol_ref, crow_ref, o_ref, *, tq):
    i = pl.program_id(1)
    scale = HEAD_DIM ** -0.5
    causal = lax.broadcasted_iota(I32, (tq, tq), 1) <= lax.broadcasted_iota(I32, (tq, tq), 0)
    for h in range(HEADS):
        q = q_ref[:, _head(h)]
        cq = ccol_ref[:, SIDE_F + h:SIDE_F + h + 1]

        def step(t, carry, masked, q=q, cq=cq, h=h):
            ks = pl.multiple_of(t * tq, tq)
            k = k_ref[pl.ds(ks, tq), _head(h)]
            v = v_ref[pl.ds(ks, tq), _head(h)]
            ck = crow_ref[t, h:h + 1, :]
            s = _dot_t(q, k) * scale + (cq - ck)
            if masked:
                s = jnp.where(causal, s, NEG)
            return _online(s, v, *carry)

        carry = lax.fori_loop(0, i, functools.partial(step, masked=False), _softmax_init(tq))
        m, l, acc = step(i, carry, True)
        o_ref[:, _head(h)] = (acc / l).astype(BF16)


def _attn_c(z3, ccol, crow, tq):
    B, S, _ = z3.shape
    return pl.pallas_call(
        functools.partial(_attn_c_kernel, tq=tq),
        grid=(B, S // tq),
        in_specs=[
            pl.BlockSpec((None, tq, TN), lambda b, i: (b, i, T_CQ)),
            pl.BlockSpec((None, S, TN), lambda b, i: (b, 0, T_CK)),
            pl.BlockSpec((None, S, TN), lambda b, i: (b, 0, T_CV)),
            pl.BlockSpec((None, tq, LANES), lambda b, i: (b, i, 0)),
            pl.BlockSpec((None, S // tq, 8, tq), lambda b, i: (b, 0, 0, 0)),
        ],
        out_specs=pl.BlockSpec((None, tq, BRANCH_W), lambda b, i: (b, i, 0)),
        out_shape=jax.ShapeDtypeStruct((B, S, BRANCH_W), BF16),
        compiler_params=_params("arbitrary", "arbitrary"),
        name="attn_forget",
    )(z3, z3, z3, ccol, crow)


def _attn_d_kernel(q_ref, iq_ref, kv_ref, side_ref, o_ref, key_ref, *, tq, k_sel):
    i = pl.program_id(1)
    n_t = i + 1
    scale = HEAD_DIM ** -0.5
    qpos = i * tq + lax.broadcasted_iota(I32, (tq, 1), 0)
    k_row = jnp.minimum(k_sel, ((qpos >> 6) + 1) * CHUNK).astype(F32)
    lane = lax.broadcasted_iota(I32, (tq, LANES), 1)
    ok = ((i * tq + lax.broadcasted_iota(I32, (tq, tq), 1)) >> 6) <= ((i * tq + lax.broadcasted_iota(I32, (tq, tq), 0)) >> 6)

    lhs, iws = [], []
    for hi in range(IDX_HEADS):
        blk = iq_ref[:, (hi // 2) * LANES:(hi // 2 + 1) * LANES]
        in_half = (lane >= IDX_DIM) if hi % 2 else (lane < IDX_DIM)
        lhs.append(jnp.where(in_half, blk, jnp.zeros_like(blk)))
        iws.append(side_ref[:, SIDE_IW + hi:SIDE_IW + hi + 1])

    def score_tile(t, masked):
        ks = pl.multiple_of(t * tq, tq)
        ik = kv_ref[pl.ds(ks, tq), MISC_IK:MISC_IK + LANES]
        sc = jnp.zeros((tq, tq), F32)
        for hi in range(IDX_HEADS):
            sc = sc + iws[hi] * jnp.maximum(_dot_t(lhs[hi], ik), 0.0)
        bits = lax.bitcast_convert_type(sc, I32)
        bits = jnp.where(bits == INT_MIN, 0, bits)
        key = jnp.where(bits < 0, bits ^ 0x7FFFFFFF, bits)
        if masked:
            key = jnp.where(ok, key, INT_MIN)
        key_ref[t] = key

    def p1(t, c):
        score_tile(t, False)
        return c

    lax.fori_loop(0, i, p1, 0)
    score_tile(i, True)

    def count(pred_fn):
        def body(t, c):
            hit = jnp.where(pred_fn(key_ref[t]), 1.0, 0.0)
            part = hit[:, 0:LANES]
            for cb in range(1, tq // LANES):
                part = part + hit[:, cb * LANES:(cb + 1) * LANES]
            return c + part

        tot = lax.fori_loop(0, n_t, body, jnp.zeros((tq, LANES), F32))
        return jnp.sum(tot, axis=-1, keepdims=True)

    def bit_step(it, ans):
        cand = ans | lax.shift_left(jnp.int32(1), 31 - it)
        cs = cand ^ INT_MIN
        cnt = count(lambda kt: kt >= cs)
        return jnp.where(cnt >= k_row, cand, ans)

    ans = lax.fori_loop(0, 32, bit_step, jnp.zeros((tq, 1), I32))
    thr = ans ^ INT_MIN
    cnt_ge = count(lambda kt: kt >= thr)
    cnt_gt = count(lambda kt: kt > thr)
    need = k_row - cnt_gt

    @pl.when(jnp.max(cnt_ge - k_row) > 0.0)
    def _():
        r = lax.broadcasted_iota(I32, (tq, tq), 0)
        c = lax.broadcasted_iota(I32, (tq, tq), 1)
        upper = jnp.where(r <= c, 1.0, 0.0).astype(BF16)

        def body(t, run):
            kt = key_ref[t]
            eq = kt == thr
            rank = run + _dot(jnp.where(eq, 1.0, 0.0).astype(BF16), upper)
            key_ref[t] = jnp.where(eq & (rank > need), INT_MIN, kt)
            return rank[:, tq - 1:tq]

        lax.fori_loop(0, n_t, body, jnp.zeros((tq, 1), F32))

    q4 = jnp.concatenate([q_ref[:, _head(h)] for h in range(HEADS)], axis=0)

    def p3(t, carry):
        ks = pl.multiple_of(t * tq, tq)
        k = kv_ref[pl.ds(ks, tq), MISC_DK:MISC_DK + LANES]
        v = kv_ref[pl.ds(ks, tq), MISC_DV:MISC_DV + LANES]
        drop = jnp.where(key_ref[t] >= thr, 0.0, NEG)
        s = _dot_t(q4, k) * scale + jnp.concatenate([drop] * HEADS, axis=0)
        return _online(s, v, *carry)

    m, l, acc = lax.fori_loop(0, n_t, p3, _softmax_init(HEADS * tq))
    o = acc / l
    for h in range(HEADS):
        o_ref[:, _head(h)] = o[h * tq:(h + 1) * tq, :].astype(BF16)


def _attn_d(z3, side3, tq, k_sel):
    B, S, _ = z3.shape
    return pl.pallas_call(
        functools.partial(_attn_d_kernel, tq=tq, k_sel=k_sel),
        grid=(B, S // tq),
        in_specs=[
            pl.BlockSpec((None, tq, TN), lambda b, i: (b, i, T_DQ)),
            pl.BlockSpec((None, tq, TN), lambda b, i: (b, i, T_DIQ)),
            pl.BlockSpec((None, S, TN), lambda b, i: (b, 0, T_MISC)),
            pl.BlockSpec((None, tq, LANES), lambda b, i: (b, i, 0)),
        ],
        out_specs=pl.BlockSpec((None, tq, BRANCH_W), lambda b, i: (b, i, 0)),
        out_shape=jax.ShapeDtypeStruct((B, S, BRANCH_W), BF16),
        scratch_shapes=[pltpu.VMEM((S // tq, tq, tq), I32)],
        compiler_params=_params("arbitrary", "arbitrary"),
        name="attn_select",
    )(z3, z3, z3, side3)


def _merge_kernel(oa_ref, ob_ref, oc_ref, od_ref, wb_ref, ga_ref, gb_ref, gc_ref, gd_ref, m_ref):
    acc = None
    for idx, (o_ref, g_ref) in enumerate(((oa_ref, ga_ref), (ob_ref, gb_ref), (oc_ref, gc_ref), (od_ref, gd_ref))):
        term = g_ref[...].astype(F32) * _dot(o_ref[...], wb_ref[idx])
        acc = term if acc is None else acc + term
    m_ref[...] = acc.astype(BF16)


def _merge(outs, wb, z, tm):
    T = z.shape[0]
    D = wb.shape[-1]
    nn = D // TN
    o_spec = pl.BlockSpec((tm, BRANCH_W), lambda i, n: (i, 0))
    g_specs = [pl.BlockSpec((tm, TN), functools.partial(lambda i, n, br: (i, T_GATE + br * nn + n), br=br))
               for br in range(N_BRANCH)]
    return pl.pallas_call(
        _merge_kernel,
        grid=(T // tm, nn),
        in_specs=[o_spec] * 4 + [pl.BlockSpec((N_BRANCH, BRANCH_W, TN), lambda i, n: (0, 0, n))] + g_specs,
        out_specs=pl.BlockSpec((tm, TN), lambda i, n: (i, n)),
        out_shape=jax.ShapeDtypeStruct((T, D), BF16),
        compiler_params=_params("arbitrary", "arbitrary"),
        name="gated_merge",
    )(*outs, wb, z, z, z, z)


def _out_proj_kernel(m_ref, w_ref, x_ref, o_ref):
    o_ref[...] = x_ref[...] + _dot(m_ref[...], w_ref[...])


def _out_proj(merged, w, x2, tm):
    T, D = x2.shape
    return pl.pallas_call(
        _out_proj_kernel,
        grid=(T // tm, D // TN),
        in_specs=[
            pl.BlockSpec((tm, D), lambda i, n: (i, 0)),
            pl.BlockSpec((D, TN), lambda i, n: (0, n)),
            pl.BlockSpec((tm, TN), lambda i, n: (i, n)),
        ],
        out_specs=pl.BlockSpec((tm, TN), lambda i, n: (i, n)),
        out_shape=jax.ShapeDtypeStruct((T, D), F32),
        compiler_params=_params("arbitrary", "arbitrary"),
        name="out_proj",
    )(merged, w, x2)


def _ffn_kernel(x_ref, g_ref, w1_ref, w2_ref, fg_ref, o_ref, h_ref, *, final_norm):
    f = pl.program_id(1)

    @pl.when(f == 0)
    def _():
        x = x_ref[...]
        y = x * lax.rsqrt(jnp.mean(x * x, axis=-1, keepdims=True) + EPS)
        h_ref[...] = (y * g_ref[...]).astype(BF16)
        o_ref[...] = x

    u = jnp.maximum(_dot(h_ref[...], w1_ref[...]), 0.0)
    o_ref[...] += _dot((u * u).astype(BF16), w2_ref[...])

    if final_norm:
        @pl.when(f == pl.num_programs(1) - 1)
        def _():
            y = o_ref[...]
            y = y * lax.rsqrt(jnp.mean(y * y, axis=-1, keepdims=True) + EPS)
            o_ref[...] = y * fg_ref[...]


def _ffn(x2, g, w1, w2, fg, tm, tf, final_norm):
    T, D = x2.shape
    dff = w1.shape[1]
    return pl.pallas_call(
        functools.partial(_ffn_kernel, final_norm=final_norm),
        grid=(T // tm, dff // tf),
        in_specs=[
            pl.BlockSpec((tm, D), lambda i, f: (i, 0)),
            pl.BlockSpec((1, D), lambda i, f: (0, 0)),
            pl.BlockSpec((D, tf), lambda i, f: (0, f)),
            pl.BlockSpec((tf, D), lambda i, f: (f, 0)),
            pl.BlockSpec((1, D), lambda i, f: (0, 0)),
        ],
        out_specs=pl.BlockSpec((tm, D), lambda i, f: (i, 0)),
        out_shape=jax.ShapeDtypeStruct((T, D), F32),
        scratch_shapes=[pltpu.VMEM((tm, D), BF16)],
        compiler_params=_params("arbitrary", "arbitrary"),
        name="ffn",
    )(x2, g, w1, w2, fg)


def _pack_w_in(w_in):
    off = {}
    o = 0
    for name, width in (("abc", 9 * BRANCH_W), ("c_f", HEADS), ("d_q", BRANCH_W), ("d_k", HEAD_DIM),
                        ("d_v", HEAD_DIM), ("d_iq", IDX_HEADS * IDX_DIM), ("d_ik", IDX_DIM),
                        ("d_iw", IDX_HEADS), ("gate", None)):
        width = w_in.shape[-1] - o if width is None else width
        off[name] = (o, o + width)
        o += width

    def seg(name):
        a, b = off[name]
        return w_in[..., a:b]

    pad = jnp.zeros(w_in.shape[:-1] + (LANES - HEADS - IDX_HEADS,), w_in.dtype)
    packed = jnp.concatenate(
        [seg("abc"), seg("d_q"), seg("d_iq"),
         seg("d_k"), seg("d_v"), seg("d_ik"), seg("d_ik"), seg("c_f"), seg("d_iw"), pad,
         seg("gate")], axis=-1)
    return packed.astype(BF16)


def _rope_table(seq):
    pos = jnp.arange(seq, dtype=F32)[:, None]

    def cs(dim):
        inv = ROPE_THETA ** (-jnp.arange(0, dim, 2, dtype=F32) / dim)
        ang = pos * inv[None, :]
        return jnp.cos(ang), jnp.sin(ang)

    c64, s64 = cs(DIFF_DIM)
    c128, s128 = cs(HEAD_DIM)
    return jnp.concatenate(
        [jnp.tile(c64, (1, 4)), jnp.tile(jnp.concatenate([-s64, s64], axis=1), (1, 2)),
         jnp.tile(c128, (1, 2)), jnp.concatenate([-s128, s128], axis=1)], axis=1)


def kernel(x, norm1_g, norm2_g, final_g, w_in, b_gate, b_forget, rel_bias, lambda_q1, lambda_k1,
           lambda_q2, lambda_k2, diff_norm_g, w_branch, w_out, w_ff1, w_ff2):
    B, S, D = x.shape
    depth = w_in.shape[0]
    T = B * S
    k_sel = min(TOPK_MAX, S // 4)
    tm = min(1024, S)
    tm_ffn = min(512, S)
    tf = 512
    tq_a = 128
    tq = 256

    w_in_p = _pack_w_in(w_in)
    wb = w_branch.astype(BF16)
    wo = w_out.astype(BF16)
    w1 = w_ff1.astype(BF16)
    w2 = w_ff2.astype(BF16)
    rope = _rope_table(S)
    zeros_pre = jnp.zeros((depth, T_GATE * TN), F32)
    bias_in = jnp.concatenate([zeros_pre, b_gate.astype(F32)], axis=1)
    bf = jnp.pad(b_forget.astype(F32), ((0, 0), (SIDE_F, LANES - SIDE_F - HEADS)))
    lamv = jnp.stack([lambda_q1, lambda_k1, lambda_q2, lambda_k2], axis=1).astype(F32)

    x2 = x.reshape(T, D)
    for l in range(depth):
        z, side = _in_proj(x2, norm1_g[l][None].astype(F32), w_in_p[l], bias_in[l][None], rope,
                           bf[l][None], S, tm)
        z3 = z.reshape(B, S, N_TILES * TN)
        side3 = side.reshape(B, S, LANES)
        ccol, crow = _cum(side3, tq)
        lambda_init = 0.8 - 0.6 * math.exp(-0.3 * l)
        oa = _attn_a(z3, _band_table(rel_bias[l], tq_a), tq_a)
        ob = _attn_b(z3, lamv[l], diff_norm_g[l][None].astype(F32), tq, lambda_init)
        oc = _attn_c(z3, ccol, crow, tq)
        od = _attn_d(z3, side3, tq, k_sel)
        outs = [o.reshape(T, BRANCH_W) for o in (oa, ob, oc, od)]
        merged = _merge(outs, wb[l], z, tm)
        x2 = _out_proj(merged, wo[l], x2, tm)
        x2 = _ffn(x2, norm2_g[l][None].astype(F32), w1[l], w2[l], final_g[None].astype(F32),
                  tm_ffn, tf, final_norm=(l == depth - 1))
    return x2.reshape(B, S, D)
```

```python
import functools
import math

import numpy as np
import jax
import jax.numpy as jnp
from jax import lax
from jax.experimental import pallas as pl
from jax.experimental.pallas import tpu as pltpu

F32 = jnp.float32
BF16 = jnp.bfloat16
I32 = jnp.int32

CHUNK = 64
N_BRANCH = 4
HEAD_DIM = 128
HEADS = 4
BRANCH_W = HEADS * HEAD_DIM
A_LEFT_CHUNKS = 8
A_LEFT = A_LEFT_CHUNKS * CHUNK
REL_CLIP = 128
DIFF_DIM = HEAD_DIM // 2
IDX_HEADS = 8
IDX_DIM = 64
TOPK_MAX = 256
ROPE_THETA = 10000.0
EPS = 1e-6

LANES = 128
SUB = 256
SUB_B = 256
LOG2E = 1.4426950408889634
NEG = -1e30
INT_MIN = -2147483648
VMEM_LIMIT = 56 * 1024 * 1024

TN = 512
T_AQ, T_AK, T_AV, T_BQ, T_BK, T_BV, T_CQ, T_CK, T_CV, T_DQ, T_DIQ, T_MISC, T_GATE = range(13)
N_TILES = T_GATE + N_BRANCH * 4
MISC_DK, MISC_DV, MISC_IK, MISC_SIDE = 0, 128, 256, 384
SIDE_F, SIDE_IW = 0, 4


def _dot(a, b):
    return jnp.dot(a, b, preferred_element_type=F32)


def _dot_t(a, b):
    return lax.dot_general(a, b, (((1,), (1,)), ((), ())), preferred_element_type=F32)


def _params(*sem):
    return pltpu.CompilerParams(dimension_semantics=sem, vmem_limit_bytes=VMEM_LIMIT)


def _head(h):
    return slice(h * HEAD_DIM, (h + 1) * HEAD_DIM)


def _rope_half(blk, cos, sin_signed, half):
    if 2 * half == LANES:
        rot = pltpu.roll(blk, half, axis=1)
    else:
        lane = lax.broadcasted_iota(I32, blk.shape, 1)
        first = (lane & (2 * half - 1)) < half
        rot = jnp.where(first, pltpu.roll(blk, LANES - half, axis=1), pltpu.roll(blk, half, axis=1))
    return blk * cos + rot * sin_signed


def _in_proj_kernel(x_ref, g_ref, w_ref, bias_ref, rope_ref, bf_ref,
                    z_ref, side_ref, sidet_ref, vtb_ref, vtc_ref, vtd_ref, h_ref, *, tk):
    j = pl.program_id(1)
    n_sub = x_ref.shape[0] // tk

    @pl.when(j == 0)
    def _():
        x = x_ref[...]
        y = x * lax.rsqrt(jnp.mean(x * x, axis=-1, keepdims=True) + EPS)
        h_ref[...] = (y * g_ref[...]).astype(BF16)

    acc = _dot(h_ref[...], w_ref[...])

    def cols(a, c):
        return a[:, c * LANES:(c + 1) * LANES]

    def rope64(blk):
        return _rope_half(blk, rope_ref[:, 0:128], rope_ref[:, 128:256], 32)

    def rope128(blk):
        return _rope_half(blk, rope_ref[:, 256:384], rope_ref[:, 384:512], 64)

    def store_t(dst_ref, a):
        for c in range(n_sub):
            dst_ref[c] = a[c * tk:(c + 1) * tk, :].T.astype(dst_ref.dtype)

    is_plain = ((j <= T_AV) | (j == T_CQ) | (j == T_CK))

    @pl.when(is_plain)
    def _():
        z_ref[...] = acc.astype(BF16)

    @pl.when(j == T_BV)
    def _():
        z_ref[...] = acc.astype(BF16)
        store_t(vtb_ref, acc)

    @pl.when(j == T_CV)
    def _():
        z_ref[...] = acc.astype(BF16)
        store_t(vtc_ref, acc)

    @pl.when((j == T_BQ) | (j == T_BK))
    def _():
        for c in range(4):
            z_ref[:, c * LANES:(c + 1) * LANES] = rope64(cols(acc, c)).astype(BF16)

    @pl.when(j == T_DQ)
    def _():
        for c in range(4):
            z_ref[:, c * LANES:(c + 1) * LANES] = rope128(cols(acc, c)).astype(BF16)

    @pl.when(j == T_DIQ)
    def _():
        for c in range(4):
            z_ref[:, c * LANES:(c + 1) * LANES] = (rope64(cols(acc, c)) * (IDX_DIM ** -0.5)).astype(BF16)

    @pl.when(j == T_MISC)
    def _():
        z_ref[:, MISC_DK:MISC_DK + LANES] = rope128(cols(acc, 0)).astype(BF16)
        dv = cols(acc, 1)
        z_ref[:, MISC_DV:MISC_DV + LANES] = dv.astype(BF16)
        store_t(vtd_ref, dv)
        z_ref[:, MISC_IK:MISC_IK + LANES] = rope64(cols(acc, 2)).astype(BF16)
        raw = cols(acc, 3)
        z_ref[:, MISC_SIDE:MISC_SIDE + LANES] = raw.astype(BF16)
        lane = lax.broadcasted_iota(I32, raw.shape, 1)
        xf = raw + bf_ref[...]
        log_f = jnp.minimum(xf, 0.0) - jnp.log(1.0 + jnp.exp(-jnp.abs(xf)))
        iw = raw * (IDX_HEADS ** -0.5)
        side = jnp.where(lane < SIDE_IW, log_f, jnp.where(lane < SIDE_IW + IDX_HEADS, iw, 0.0))
        side_ref[...] = side
        store_t(sidet_ref, side)

    @pl.when(j >= T_GATE)
    def _():
        z_ref[...] = (1.0 / (1.0 + jnp.exp(-(acc + bias_ref[...])))).astype(BF16)


def _in_proj(x2, g, w, bias, rope, bf, seq, tm, tk):
    T, D = x2.shape
    n_pos = seq // tm
    n_sub = tm // tk

    def t_spec(rows):
        return pl.BlockSpec((n_sub, rows, tk), lambda i, j: (i, 0, 0))

    return pl.pallas_call(
        functools.partial(_in_proj_kernel, tk=tk),
        grid=(T // tm, N_TILES),
        in_specs=[
            pl.BlockSpec((tm, D), lambda i, j: (i, 0)),
            pl.BlockSpec((1, D), lambda i, j: (0, 0)),
            pl.BlockSpec((D, TN), lambda i, j: (0, j)),
            pl.BlockSpec((1, TN), lambda i, j: (0, j)),
            pl.BlockSpec((tm, TN), lambda i, j: (i % n_pos, 0)),
            pl.BlockSpec((1, LANES), lambda i, j: (0, 0)),
        ],
        out_specs=[
            pl.BlockSpec((tm, TN), lambda i, j: (i, j)),
            pl.BlockSpec((tm, LANES), lambda i, j: (i, 0)),
            t_spec(LANES), t_spec(BRANCH_W), t_spec(BRANCH_W), t_spec(HEAD_DIM),
        ],
        out_shape=[
            jax.ShapeDtypeStruct((T, N_TILES * TN), BF16),
            jax.ShapeDtypeStruct((T, LANES), F32),
            jax.ShapeDtypeStruct((T // tk, LANES, tk), F32),
            jax.ShapeDtypeStruct((T // tk, BRANCH_W, tk), BF16),
            jax.ShapeDtypeStruct((T // tk, BRANCH_W, tk), BF16),
            jax.ShapeDtypeStruct((T // tk, HEAD_DIM, tk), BF16),
        ],
        scratch_shapes=[pltpu.VMEM((tm, D), BF16)],
        compiler_params=_params("arbitrary", "arbitrary"),
        name="in_proj",
    )(x2, g, w, bias, rope, bf)


def _split3(x):
    hi = x.astype(BF16)
    r1 = x - hi.astype(F32)
    mid = r1.astype(BF16)
    lo = (r1 - mid.astype(F32)).astype(BF16)
    return hi, mid, lo


def _cum_kernel(side_ref, ka_ref, qa_ref, *, blk):
    S = side_ref.shape[0]
    r = lax.broadcasted_iota(I32, (blk, blk), 0)
    c = lax.broadcasted_iota(I32, (blk, blk), 1)
    tri = jnp.where(c <= r, 1.0, 0.0).astype(BF16)
    lane = lax.broadcasted_iota(I32, (blk, HEAD_DIM), 1)
    carry = jnp.zeros((1, LANES), F32)
    for b in range(S // blk):
        rows = slice(b * blk, (b + 1) * blk)
        hi, mid, lo = _split3(side_ref[rows, :])
        cb = (_dot(tri, hi) + _dot(tri, mid)) + _dot(tri, lo) + carry
        carry = cb[blk - 1:blk, :]
        for h in range(HEADS):
            parts = _split3(cb[:, SIDE_F + h:SIDE_F + h + 1] * (HEAD_DIM ** 0.5))
            ka = jnp.where((lane >= 3) & (lane < 6), 1.0, 0.0)
            qa = jnp.where(lane < 3, 1.0, 0.0)
            for n, part in enumerate(parts):
                ka = jnp.where(lane == n, -part.astype(F32), ka)
                qa = jnp.where(lane == 3 + n, part.astype(F32), qa)
            ka_ref[rows, _head(h)] = ka.astype(BF16)
            qa_ref[rows, _head(h)] = qa.astype(BF16)


def _cum(side3, blk):
    B, S, _ = side3.shape
    spec = pl.BlockSpec((None, S, BRANCH_W), lambda b: (b, 0, 0))
    shape = jax.ShapeDtypeStruct((B, S, BRANCH_W), BF16)
    return pl.pallas_call(
        functools.partial(_cum_kernel, blk=blk),
        grid=(B,),
        in_specs=[pl.BlockSpec((None, S, LANES), lambda b: (b, 0, 0))],
        out_specs=[spec, spec],
        out_shape=[shape, shape],
        compiler_params=_params("arbitrary"),
        name="forget_cumsum",
    )(side3)


def _online_multi(sts, vts, carries):
    stats = []
    for st, (m, l, acc) in zip(sts, carries):
        m_new = jnp.maximum(m, jnp.max(st, axis=0, keepdims=True))
        alpha = jnp.exp2(m - m_new)
        p = jnp.exp2(st - m_new)
        stats.append((m_new, alpha, alpha * l + jnp.sum(p, axis=0, keepdims=True), p.astype(BF16)))
    out = []
    for vt, (m_new, alpha, l, p), (_, _, acc) in zip(vts, stats, carries):
        out.append((m_new, l, alpha * acc + _dot(vt, p)))
    return out


def _online_t(st, vt, m, l, acc):
    return _online_multi([st], [vt], [(m, l, acc)])[0]


def _softmax_init_t(cols):
    return (jnp.full((1, cols), NEG, F32), jnp.zeros((1, cols), F32), jnp.zeros((HEAD_DIM, cols), F32))


def _attn_a_kernel(q_ref, k_ref, v_ref, bias_ref, o_ref, *, tq, win):
    i = pl.program_id(1)
    start = pl.multiple_of(jnp.maximum(i * tq - A_LEFT, 0), tq)
    scale = HEAD_DIM ** -0.5
    for h in range(HEADS):
        q = q_ref[:, _head(h)]
        k = k_ref[pl.ds(start, win), _head(h)]
        v = v_ref[pl.ds(start, win), _head(h)]
        s = _dot_t(q, k) * scale + bias_ref[h]
        m = jnp.max(s, axis=-1, keepdims=True)
        p = jnp.exp(s - m)
        l = jnp.sum(p, axis=-1, keepdims=True)
        o_ref[:, _head(h)] = (_dot(p.astype(BF16), v) / l).astype(BF16)


def _attn_a(z3, table, tq):
    B, S, _ = z3.shape
    nvar, _, _, win = table.shape
    return pl.pallas_call(
        functools.partial(_attn_a_kernel, tq=tq, win=win),
        grid=(B, S // tq),
        in_specs=[
            pl.BlockSpec((None, tq, TN), lambda b, i: (b, i, T_AQ)),
            pl.BlockSpec((None, S, TN), lambda b, i: (b, 0, T_AK)),
            pl.BlockSpec((None, S, TN), lambda b, i: (b, 0, T_AV)),
            pl.BlockSpec((None, HEADS, tq, win), lambda b, i: (jnp.minimum(i, nvar - 1), 0, 0, 0)),
        ],
        out_specs=pl.BlockSpec((None, tq, BRANCH_W), lambda b, i: (b, i, 0)),
        out_shape=jax.ShapeDtypeStruct((B, S, BRANCH_W), BF16),
        compiler_params=_params("arbitrary", "arbitrary"),
        name="attn_band",
    )(z3, z3, z3, table)


def _band_table(rel_bias_l, tq):
    H = rel_bias_l.shape[0]
    win = A_LEFT + tq
    nvar = A_LEFT // tq + 1
    wfull = win + A_LEFT
    r0 = A_LEFT + tq - 1
    length = wfull + tq - 1
    rb = rel_bias_l.astype(F32)
    g = jnp.concatenate(
        [jnp.broadcast_to(rb[:, 2 * REL_CLIP:], (H, r0 - REL_CLIP)),
         rb[:, ::-1],
         jnp.broadcast_to(rb[:, :1], (H, length - (r0 - REL_CLIP) - (2 * REL_CLIP + 1)))], axis=1)
    gp = jnp.pad(g, ((0, 0), (0, 1)))
    skew = jnp.tile(gp, (1, tq))[:, :tq * length].reshape(H, tq, length)
    full = skew[:, :, tq - 1:tq - 1 + wfull]

    qi = np.arange(tq)[:, None]
    kj = np.arange(win)[None, :]
    tables = []
    for v in range(nvar):
        d = A_LEFT - v * tq
        kk = kj + d
        cq = (A_LEFT + qi) // CHUNK
        ck = kk // CHUNK
        valid = (ck <= cq) & (ck >= cq - A_LEFT_CHUNKS)
        tables.append(jnp.where(valid[None], full[:, :, d:d + win], NEG))
    return jnp.stack(tables)


def _attn_b_kernel(q_ref, k_ref, vt_ref, lam_ref, g_ref, o_ref, *, tq, lambda_init):
    i = pl.program_id(1)
    scale = DIFF_DIM ** -0.5 * LOG2E
    lv = lam_ref[...]
    lam = (jnp.exp(jnp.sum(lv[0:1] * lv[1:2], axis=-1, keepdims=True))
           - jnp.exp(jnp.sum(lv[2:3] * lv[3:4], axis=-1, keepdims=True)) + lambda_init)
    lane = lax.broadcasted_iota(I32, (tq, HEAD_DIM), 1)
    qq = []
    for h in range(HEADS):
        q = q_ref[:, _head(h)]
        zero = jnp.zeros_like(q)
        qq.append(jnp.concatenate([jnp.where(lane < DIFF_DIM, q, zero), jnp.where(lane >= DIFF_DIM, q, zero)], axis=0))
    kchunk = lax.broadcasted_iota(I32, (SUB_B, 2 * tq), 0) >> 6
    qchunk = (lax.broadcasted_iota(I32, (SUB_B, 2 * tq), 1) & (tq - 1)) >> 6

    def step(t, carry, masked):
        out = list(carry)
        for sub in range(tq // SUB_B):
            ks = pl.multiple_of(t * tq + sub * SUB_B, SUB_B)
            sts, vts = [], []
            for h in range(HEADS):
                k = k_ref[pl.ds(ks, SUB_B), _head(h)]
                vts.append(vt_ref[t, _head(h), sub * SUB_B:(sub + 1) * SUB_B])
                s = _dot_t(k, qq[h]) * scale
                if masked:
                    s = jnp.where(kchunk + (sub * SUB_B) // CHUNK <= qchunk, s, NEG)
                sts.append(s)
            out = _online_multi(sts, vts, out)
        return tuple(out)

    init = tuple(_softmax_init_t(2 * tq) for _ in range(HEADS))
    carry = lax.fori_loop(0, i, functools.partial(step, masked=False), init)
    carry = step(i, carry, True)
    for h in range(HEADS):
        m, l, acc = carry[h]
        on = acc / l
        o = (on[:, :tq] - lam * on[:, tq:]).T
        y = o * lax.rsqrt(jnp.mean(o * o, axis=-1, keepdims=True) + EPS)
        o_ref[:, _head(h)] = ((y * g_ref[...]) * (1.0 - lambda_init)).astype(BF16)


def _attn_b(z3, vt, lamv, g, tq, lambda_init):
    B, S, _ = z3.shape
    nt = S // tq
    return pl.pallas_call(
        functools.partial(_attn_b_kernel, tq=tq, lambda_init=lambda_init),
        grid=(B, nt),
        in_specs=[
            pl.BlockSpec((None, tq, TN), lambda b, i: (b, i, T_BQ)),
            pl.BlockSpec((None, S, TN), lambda b, i: (b, 0, T_BK)),
            pl.BlockSpec((nt, BRANCH_W, tq), lambda b, i: (b, 0, 0)),
            pl.BlockSpec((4, DIFF_DIM), lambda b, i: (0, 0)),
            pl.BlockSpec((1, HEAD_DIM), lambda b, i: (0, 0)),
        ],
        out_specs=pl.BlockSpec((None, tq, BRANCH_W), lambda b, i: (b, i, 0)),
        out_shape=jax.ShapeDtypeStruct((B, S, BRANCH_W), BF16),
        compiler_params=_params("arbitrary", "arbitrary"),
        name="attn_diff",
    )(z3, z3, vt, lamv, g)


def _attn_c_kernel(q_ref, k_ref, vt_ref, ka_ref, qa_ref, o_ref, *, tq):
    i = pl.program_id(1)
    scale = HEAD_DIM ** -0.5 * LOG2E
    q_aug = [jnp.concatenate([q_ref[:, _head(h)], qa_ref[:, _head(h)]], axis=1) for h in range(HEADS)]
    krow = lax.broadcasted_iota(I32, (SUB, tq), 0)
    qcol = lax.broadcasted_iota(I32, (SUB, tq), 1)

    def step(t, carry, masked):
        out = list(carry)
        for sub in range(tq // SUB):
            ks = pl.multiple_of(t * tq + sub * SUB, SUB)
            sts, vts = [], []
            for h in range(HEADS):
                k_aug = jnp.concatenate([k_ref[pl.ds(ks, SUB), _head(h)], ka_ref[pl.ds(ks, SUB), _head(h)]], axis=1)
                vts.append(vt_ref[t, _head(h), sub * SUB:(sub + 1) * SUB])
                s = _dot_t(k_aug, q_aug[h]) * scale
                if masked:
                    s = jnp.where(krow + sub * SUB <= qcol, s, NEG)
                sts.append(s)
            out = _online_multi(sts, vts, out)
        return tuple(out)

    init = tuple(_softmax_init_t(tq) for _ in range(HEADS))
    carry = lax.fori_loop(0, i, functools.partial(step, masked=False), init)
    carry = step(i, carry, True)
    for h in range(HEADS):
        m, l, acc = carry[h]
        o_ref[:, _head(h)] = (acc / l).T.astype(BF16)


def _attn_c(z3, vt, ka, qa, tq):
    B, S, _ = z3.shape
    nt = S // tq
    return pl.pallas_call(
        functools.partial(_attn_c_kernel, tq=tq),
        grid=(B, nt),
        in_specs=[
            pl.BlockSpec((None, tq, TN), lambda b, i: (b, i, T_CQ)),
            pl.BlockSpec((None, S, TN), lambda b, i: (b, 0, T_CK)),
            pl.BlockSpec((nt, BRANCH_W, tq), lambda b, i: (b, 0, 0)),
            pl.BlockSpec((None, S, BRANCH_W), lambda b, i: (b, 0, 0)),
            pl.BlockSpec((None, tq, BRANCH_W), lambda b, i: (b, i, 0)),
        ],
        out_specs=pl.BlockSpec((None, tq, BRANCH_W), lambda b, i: (b, i, 0)),
        out_shape=jax.ShapeDtypeStruct((B, S, BRANCH_W), BF16),
        compiler_params=_params("arbitrary", "arbitrary"),
        name="attn_forget",
    )(z3, z3, vt, ka, qa)


def _attn_d_kernel(q_ref, iq_ref, kv_ref, vt_ref, sidet_ref, o_ref, sc_ref, *, tq, k_sel):
    i = pl.program_id(1)
    n_t = i + 1
    scale = HEAD_DIM ** -0.5 * LOG2E
    ninf = float("-inf")
    qpos = i * tq + lax.broadcasted_iota(I32, (1, tq), 1)
    k_row = jnp.minimum(k_sel, ((qpos >> 6) + 1) * CHUNK).astype(F32)
    lane = lax.broadcasted_iota(I32, (tq, LANES), 1)
    ok = ((i * tq + lax.broadcasted_iota(I32, (tq, tq), 0)) >> 6) <= ((i * tq + lax.broadcasted_iota(I32, (tq, tq), 1)) >> 6)

    lhs, iws = [], []
    for hi in range(IDX_HEADS):
        blk = iq_ref[:, (hi // 2) * LANES:(hi // 2 + 1) * LANES]
        in_half = (lane >= IDX_DIM) if hi % 2 else (lane < IDX_DIM)
        lhs.append(jnp.where(in_half, blk, jnp.zeros_like(blk)))
        iws.append(sidet_ref[SIDE_IW + hi:SIDE_IW + hi + 1, :])

    def score_tile(t, masked):
        ks = pl.multiple_of(t * tq, tq)
        ik = kv_ref[pl.ds(ks, tq), MISC_IK:MISC_IK + LANES]
        sc = iws[0] * jnp.maximum(_dot_t(ik, lhs[0]), 0.0)
        for hi in range(1, IDX_HEADS):
            sc = sc + iws[hi] * jnp.maximum(_dot_t(ik, lhs[hi]), 0.0)
        if masked:
            sc = jnp.where(ok, sc, ninf)
        sc_ref[t] = sc

    def p1(t, c):
        score_tile(t, False)
        return c

    lax.fori_loop(0, i, p1, 0)
    score_tile(i, True)
    sc_ref[n_t] = jnp.full((tq, tq), ninf, F32)

    def count(pred_fn):
        def body(p, c):
            for t in (2 * p, 2 * p + 1):
                c = c + jnp.sum(jnp.where(pred_fn(sc_ref[t]), 1.0, 0.0), axis=0, keepdims=True)
            return c

        return lax.fori_loop(0, (n_t + 1) >> 1, body, jnp.zeros((1, tq), F32))

    def as_float(u):
        key = u ^ INT_MIN
        return lax.bitcast_convert_type(jnp.where(key < 0, key ^ 0x7FFFFFFF, key), F32)

    def bit_step(it, ans):
        cand = ans | lax.shift_left(jnp.int32(1), 31 - it)
        cf = as_float(cand)
        cnt = count(lambda st: st >= cf)
        return jnp.where(cnt >= k_row, cand, ans)

    thr = as_float(lax.fori_loop(0, 32, bit_step, jnp.zeros((1, tq), I32)))
    cnt_ge = count(lambda st: st >= thr)

    @pl.when(jnp.max(cnt_ge - k_row) > 0.0)
    def _():
        need = k_row - count(lambda st: st > thr)
        r = lax.broadcasted_iota(I32, (tq, tq), 0)
        c = lax.broadcasted_iota(I32, (tq, tq), 1)
        lower = jnp.where(c <= r, 1.0, 0.0).astype(BF16)

        def body(t, run):
            st = sc_ref[t]
            eq = st == thr
            rank = run + _dot(lower, jnp.where(eq, 1.0, 0.0).astype(BF16))
            sc_ref[t] = jnp.where(eq & (rank > need), ninf, st)
            return rank[tq - 1:tq, :]

        lax.fori_loop(0, n_t, body, jnp.zeros((1, tq), F32))

    q4 = jnp.concatenate([q_ref[:, _head(h)] for h in range(HEADS)], axis=0)

    def p3(t, carry):
        for sub in range(tq // SUB):
            ks = pl.multiple_of(t * tq + sub * SUB, SUB)
            k = kv_ref[pl.ds(ks, SUB), MISC_DK:MISC_DK + LANES]
            vt = vt_ref[t, :, sub * SUB:(sub + 1) * SUB]
            sel = sc_ref[t, sub * SUB:(sub + 1) * SUB, :] >= thr
            s = _dot_t(k, q4) * scale
            s = jnp.concatenate([jnp.where(sel, s[:, h * tq:(h + 1) * tq], NEG) for h in range(HEADS)], axis=1)
            carry = _online_t(s, vt, *carry)
        return carry

    m, l, acc = lax.fori_loop(0, n_t, p3, _softmax_init_t(HEADS * tq))
    o = acc / l
    for h in range(HEADS):
        o_ref[:, _head(h)] = o[:, h * tq:(h + 1) * tq].T.astype(BF16)


def _attn_d(z3, vt, sidet, tq, k_sel):
    B, S, _ = z3.shape
    nt = S // tq
    return pl.pallas_call(
        functools.partial(_attn_d_kernel, tq=tq, k_sel=k_sel),
        grid=(B, nt),
        in_specs=[
            pl.BlockSpec((None, tq, TN), lambda b, i: (b, i, T_DQ)),
            pl.BlockSpec((None, tq, TN), lambda b, i: (b, i, T_DIQ)),
            pl.BlockSpec((None, S, TN), lambda b, i: (b, 0, T_MISC)),
            pl.BlockSpec((nt, HEAD_DIM, tq), lambda b, i: (b, 0, 0)),
            pl.BlockSpec((None, LANES, tq), lambda b, i: (b * nt + i, 0, 0)),
        ],
        out_specs=pl.BlockSpec((None, tq, BRANCH_W), lambda b, i: (b, i, 0)),
        out_shape=jax.ShapeDtypeStruct((B, S, BRANCH_W), BF16),
        scratch_shapes=[pltpu.VMEM((nt + 1, tq, tq), F32)],
        compiler_params=_params("arbitrary", "arbitrary"),
        name="attn_select",
    )(z3, z3, z3, vt, sidet)


def _merge_kernel(oa_ref, ob_ref, oc_ref, od_ref, wb_ref, ga_ref, gb_ref, gc_ref, gd_ref, m_ref):
    acc = None
    for idx, (o_ref, g_ref) in enumerate(((oa_ref, ga_ref), (ob_ref, gb_ref), (oc_ref, gc_ref), (od_ref, gd_ref))):
        term = g_ref[...].astype(F32) * _dot(o_ref[...], wb_ref[idx])
        acc = term if acc is None else acc + term
    m_ref[...] = acc.astype(BF16)


def _merge(outs, wb, z, tm):
    T = z.shape[0]
    D = wb.shape[-1]
    nn = D // TN
    o_spec = pl.BlockSpec((tm, BRANCH_W), lambda i, n: (i, 0))
    g_specs = [pl.BlockSpec((tm, TN), functools.partial(lambda i, n, br: (i, T_GATE + br * nn + n), br=br))
               for br in range(N_BRANCH)]
    return pl.pallas_call(
        _merge_kernel,
        grid=(T // tm, nn),
        in_specs=[o_spec] * 4 + [pl.BlockSpec((N_BRANCH, BRANCH_W, TN), lambda i, n: (0, 0, n))] + g_specs,
        out_specs=pl.BlockSpec((tm, TN), lambda i, n: (i, n)),
        out_shape=jax.ShapeDtypeStruct((T, D), BF16),
        compiler_params=_params("arbitrary", "arbitrary"),
        name="gated_merge",
    )(*outs, wb, z, z, z, z)


def _out_proj_kernel(m_ref, w_ref, x_ref, o_ref):
    o_ref[...] = x_ref[...] + _dot(m_ref[...], w_ref[...])


def _out_proj(merged, w, x2, tm):
    T, D = x2.shape
    return pl.pallas_call(
        _out_proj_kernel,
        grid=(T // tm, D // TN),
        in_specs=[
            pl.BlockSpec((tm, D), lambda i, n: (i, 0)),
            pl.BlockSpec((D, TN), lambda i, n: (0, n)),
            pl.BlockSpec((tm, TN), lambda i, n: (i, n)),
        ],
        out_specs=pl.BlockSpec((tm, TN), lambda i, n: (i, n)),
        out_shape=jax.ShapeDtypeStruct((T, D), F32),
        compiler_params=_params("arbitrary", "arbitrary"),
        name="out_proj",
    )(merged, w, x2)


def _ffn_kernel(x_ref, g_ref, w1_ref, w2_ref, fg_ref, o_ref, h_ref, *, final_norm):
    f = pl.program_id(1)

    @pl.when(f == 0)
    def _():
        x = x_ref[...]
        y = x * lax.rsqrt(jnp.mean(x * x, axis=-1, keepdims=True) + EPS)
        h_ref[...] = (y * g_ref[...]).astype(BF16)
        o_ref[...] = x

    u = jnp.maximum(_dot(h_ref[...], w1_ref[...]), 0.0)
    o_ref[...] += _dot((u * u).astype(BF16), w2_ref[...])

    if final_norm:
        @pl.when(f == pl.num_programs(1) - 1)
        def _():
            y = o_ref[...]
            y = y * lax.rsqrt(jnp.mean(y * y, axis=-1, keepdims=True) + EPS)
            o_ref[...] = y * fg_ref[...]


def _ffn(x2, g, w1, w2, fg, tm, tf, final_norm):
    T, D = x2.shape
    dff = w1.shape[1]
    return pl.pallas_call(
        functools.partial(_ffn_kernel, final_norm=final_norm),
        grid=(T // tm, dff // tf),
        in_specs=[
            pl.BlockSpec((tm, D), lambda i, f: (i, 0)),
            pl.BlockSpec((1, D), lambda i, f: (0, 0)),
            pl.BlockSpec((D, tf), lambda i, f: (0, f)),
            pl.BlockSpec((tf, D), lambda i, f: (f, 0)),
            pl.BlockSpec((1, D), lambda i, f: (0, 0)),
        ],
        out_specs=pl.BlockSpec((tm, D), lambda i, f: (i, 0)),
        out_shape=jax.ShapeDtypeStruct((T, D), F32),
        scratch_shapes=[pltpu.VMEM((tm, D), BF16)],
        compiler_params=_params("arbitrary", "arbitrary"),
        name="ffn",
    )(x2, g, w1, w2, fg)


def _pack_w_in(w_in):
    off = {}
    o = 0
    for name, width in (("abc", 9 * BRANCH_W), ("c_f", HEADS), ("d_q", BRANCH_W), ("d_k", HEAD_DIM),
                        ("d_v", HEAD_DIM), ("d_iq", IDX_HEADS * IDX_DIM), ("d_ik", IDX_DIM),
                        ("d_iw", IDX_HEADS), ("gate", None)):
        width = w_in.shape[-1] - o if width is None else width
        off[name] = (o, o + width)
        o += width

    def seg(name):
        a, b = off[name]
        return w_in[..., a:b]

    pad = jnp.zeros(w_in.shape[:-1] + (LANES - HEADS - IDX_HEADS,), w_in.dtype)
    packed = jnp.concatenate(
        [seg("abc"), seg("d_q"), seg("d_iq"),
         seg("d_k"), seg("d_v"), seg("d_ik"), seg("d_ik"), seg("c_f"), seg("d_iw"), pad,
         seg("gate")], axis=-1)
    return packed.astype(BF16)


def _rope_table(seq):
    pos = jnp.arange(seq, dtype=F32)[:, None]

    def cs(dim):
        inv = ROPE_THETA ** (-jnp.arange(0, dim, 2, dtype=F32) / dim)
        ang = pos * inv[None, :]
        return jnp.cos(ang), jnp.sin(ang)

    c64, s64 = cs(DIFF_DIM)
    c128, s128 = cs(HEAD_DIM)
    return jnp.concatenate(
        [jnp.tile(c64, (1, 4)), jnp.tile(jnp.concatenate([-s64, s64], axis=1), (1, 2)),
         jnp.tile(c128, (1, 2)), jnp.concatenate([-s128, s128], axis=1)], axis=1)


def kernel(x, norm1_g, norm2_g, final_g, w_in, b_gate, b_forget, rel_bias, lambda_q1, lambda_k1,
           lambda_q2, lambda_k2, diff_norm_g, w_branch, w_out, w_ff1, w_ff2):
    B, S, D = x.shape
    depth = w_in.shape[0]
    T = B * S
    k_sel = min(TOPK_MAX, S // 4)
    tm = min(1024, S)
    tm_ffn = min(512, S)
    tf = 512
    tq_a = 128
    tq = 256

    w_in_p = _pack_w_in(w_in)
    wb = w_branch.astype(BF16)
    wo = w_out.astype(BF16)
    w1 = w_ff1.astype(BF16)
    w2 = w_ff2.astype(BF16)
    rope = _rope_table(S)
    zeros_pre = jnp.zeros((depth, T_GATE * TN), F32)
    bias_in = jnp.concatenate([zeros_pre, b_gate.astype(F32)], axis=1)
    bf = jnp.pad(b_forget.astype(F32), ((0, 0), (SIDE_F, LANES - SIDE_F - HEADS)))
    lamv = jnp.stack([lambda_q1, lambda_k1, lambda_q2, lambda_k2], axis=1).astype(F32)

    x2 = x.reshape(T, D)
    for l in range(depth):
        z, side, sidet, vtb, vtc, vtd = _in_proj(
            x2, norm1_g[l][None].astype(F32), w_in_p[l], bias_in[l][None], rope, bf[l][None], S, tm, tq)
        z3 = z.reshape(B, S, N_TILES * TN)
        ka, qa = _cum(side.reshape(B, S, LANES), tq)
        lambda_init = 0.8 - 0.6 * math.exp(-0.3 * l)
        oa = _attn_a(z3, _band_table(rel_bias[l], tq_a), tq_a)
        ob = _attn_b(z3, vtb, lamv[l], diff_norm_g[l][None].astype(F32), tq, lambda_init)
        oc = _attn_c(z3, vtc, ka, qa, tq)
        od = _attn_d(z3, vtd, sidet, tq, k_sel)
        outs = [o.reshape(T, BRANCH_W) for o in (oa, ob, oc, od)]
        merged = _merge(outs, wb[l], z, tm)
        x2 = _out_proj(merged, wo[l], x2, tm)
        x2 = _ffn(x2, norm2_g[l][None].astype(F32), w1[l], w2[l], final_g[None].astype(F32),
                  tm_ffn, tf, final_norm=(l == depth - 1))
    return x2.reshape(B, S, D)
```

```python
import functools
import math

import numpy as np
import jax
import jax.numpy as jnp
from jax import lax
from jax.experimental import pallas as pl
from jax.experimental.pallas import tpu as pltpu

F32 = jnp.float32
BF16 = jnp.bfloat16
I32 = jnp.int32

CHUNK = 64
N_BRANCH = 4
HEAD_DIM = 128
HEADS = 4
BRANCH_W = HEADS * HEAD_DIM
A_LEFT_CHUNKS = 8
A_LEFT = A_LEFT_CHUNKS * CHUNK
REL_CLIP = 128
DIFF_DIM = HEAD_DIM // 2
IDX_HEADS = 8
IDX_DIM = 64
TOPK_MAX = 256
ROPE_THETA = 10000.0
EPS = 1e-6

LANES = 128
LOG2E = 1.4426950408889634
NEG = -1e30
INT_MIN = -2147483648
VMEM_LIMIT = 56 * 1024 * 1024

TN = 512
T_AQ, T_AK, T_AV, T_BQ, T_BK, T_BV, T_CQ, T_CK, T_CV, T_DQ, T_DIQ, T_MISC, T_GATE = range(13)
N_TILES = T_GATE + N_BRANCH * 4
MISC_DK, MISC_DV, MISC_IK, MISC_SIDE = 0, 128, 256, 384
SIDE_F, SIDE_IW = 0, 4


def _dot(a, b):
    return jnp.dot(a, b, preferred_element_type=F32)


def _dot_t(a, b):
    return lax.dot_general(a, b, (((1,), (1,)), ((), ())), preferred_element_type=F32)


def _params(*sem):
    return pltpu.CompilerParams(dimension_semantics=sem, vmem_limit_bytes=VMEM_LIMIT)


def _head(h):
    return slice(h * HEAD_DIM, (h + 1) * HEAD_DIM)


def _rope_half(blk, cos, sin_signed, half):
    if 2 * half == LANES:
        rot = pltpu.roll(blk, half, axis=1)
    else:
        lane = lax.broadcasted_iota(I32, blk.shape, 1)
        first = (lane & (2 * half - 1)) < half
        rot = jnp.where(first, pltpu.roll(blk, LANES - half, axis=1), pltpu.roll(blk, half, axis=1))
    return blk * cos + rot * sin_signed


def _in_proj_kernel(x_ref, g_ref, w_ref, bias_ref, rope_ref, bf_ref,
                    z_ref, side_ref, sidet_ref, vtb_ref, vtc_ref, vtd_ref, h_ref, *, tk):
    j = pl.program_id(1)
    n_sub = x_ref.shape[0] // tk

    @pl.when(j == 0)
    def _():
        x = x_ref[...]
        y = x * lax.rsqrt(jnp.mean(x * x, axis=-1, keepdims=True) + EPS)
        h_ref[...] = (y * g_ref[...]).astype(BF16)

    acc = _dot_t(h_ref[...], w_ref[...])

    def cols(a, c):
        return a[:, c * LANES:(c + 1) * LANES]

    def rope64(blk):
        return _rope_half(blk, rope_ref[:, 0:128], rope_ref[:, 128:256], 32)

    def rope128(blk):
        return _rope_half(blk, rope_ref[:, 256:384], rope_ref[:, 384:512], 64)

    def store_t(dst_ref, a):
        for c in range(n_sub):
            dst_ref[c] = a[c * tk:(c + 1) * tk, :].T.astype(dst_ref.dtype)

    is_plain = ((j <= T_AV) | (j == T_CQ) | (j == T_CK))

    @pl.when(is_plain)
    def _():
        z_ref[...] = acc.astype(BF16)

    @pl.when(j == T_BV)
    def _():
        z_ref[...] = acc.astype(BF16)
        store_t(vtb_ref, acc)

    @pl.when(j == T_CV)
    def _():
        z_ref[...] = acc.astype(BF16)
        store_t(vtc_ref, acc)

    @pl.when((j == T_BQ) | (j == T_BK))
    def _():
        for c in range(4):
            z_ref[:, c * LANES:(c + 1) * LANES] = rope64(cols(acc, c)).astype(BF16)

    @pl.when(j == T_DQ)
    def _():
        for c in range(4):
            z_ref[:, c * LANES:(c + 1) * LANES] = rope128(cols(acc, c)).astype(BF16)

    @pl.when(j == T_DIQ)
    def _():
        for c in range(4):
            z_ref[:, c * LANES:(c + 1) * LANES] = (rope64(cols(acc, c)) * (IDX_DIM ** -0.5)).astype(BF16)

    @pl.when(j == T_MISC)
    def _():
        z_ref[:, MISC_DK:MISC_DK + LANES] = rope128(cols(acc, 0)).astype(BF16)
        dv = cols(acc, 1)
        z_ref[:, MISC_DV:MISC_DV + LANES] = dv.astype(BF16)
        store_t(vtd_ref, dv)
        z_ref[:, MISC_IK:MISC_IK + LANES] = rope64(cols(acc, 2)).astype(BF16)
        raw = cols(acc, 3)
        z_ref[:, MISC_SIDE:MISC_SIDE + LANES] = raw.astype(BF16)
        lane = lax.broadcasted_iota(I32, raw.shape, 1)
        xf = raw + bf_ref[...]
        log_f = jnp.minimum(xf, 0.0) - jnp.log(1.0 + jnp.exp(-jnp.abs(xf)))
        iw = raw * (IDX_HEADS ** -0.5)
        side = jnp.where(lane < SIDE_IW, log_f, jnp.where(lane < SIDE_IW + IDX_HEADS, iw, 0.0))
        side_ref[...] = side
        store_t(sidet_ref, side)

    @pl.when(j >= T_GATE)
    def _():
        z_ref[...] = (1.0 / (1.0 + jnp.exp(-(acc + bias_ref[...])))).astype(BF16)


def _in_proj(x2, g, w, bias, rope, bf, layer, seq, tm, tk):
    T, D = x2.shape
    n_pos = seq // tm
    n_sub = tm // tk

    def t_spec(rows):
        return pl.BlockSpec((n_sub, rows, tk), lambda i, j: (i, 0, 0))

    return pl.pallas_call(
        functools.partial(_in_proj_kernel, tk=tk),
        grid=(T // tm, N_TILES),
        in_specs=[
            pl.BlockSpec((tm, D), lambda i, j: (i, 0)),
            pl.BlockSpec((1, D), lambda i, j: (0, 0)),
            pl.BlockSpec((TN, D), lambda i, j: (j, layer)),
            pl.BlockSpec((None, 1, TN), lambda i, j: (layer, 0, j)),
            pl.BlockSpec((tm, TN), lambda i, j: (i % n_pos, 0)),
            pl.BlockSpec((1, LANES), lambda i, j: (0, 0)),
        ],
        out_specs=[
            pl.BlockSpec((tm, TN), lambda i, j: (i, j)),
            pl.BlockSpec((tm, LANES), lambda i, j: (i, 0)),
            t_spec(LANES), t_spec(BRANCH_W), t_spec(BRANCH_W), t_spec(HEAD_DIM),
        ],
        out_shape=[
            jax.ShapeDtypeStruct((T, N_TILES * TN), BF16),
            jax.ShapeDtypeStruct((T, LANES), F32),
            jax.ShapeDtypeStruct((T // tk, LANES, tk), F32),
            jax.ShapeDtypeStruct((T // tk, BRANCH_W, tk), BF16),
            jax.ShapeDtypeStruct((T // tk, BRANCH_W, tk), BF16),
            jax.ShapeDtypeStruct((T // tk, HEAD_DIM, tk), BF16),
        ],
        scratch_shapes=[pltpu.VMEM((tm, D), BF16)],
        compiler_params=_params("arbitrary", "arbitrary"),
        name="in_proj",
    )(x2, g, w, bias, rope, bf)


def _split3(x):
    hi = x.astype(BF16)
    r1 = x - hi.astype(F32)
    mid = r1.astype(BF16)
    lo = (r1 - mid.astype(F32)).astype(BF16)
    return hi, mid, lo


def _cum_kernel(side_ref, ka_ref, qa_ref, *, blk):
    S = side_ref.shape[0]
    r = lax.broadcasted_iota(I32, (blk, blk), 0)
    c = lax.broadcasted_iota(I32, (blk, blk), 1)
    tri = jnp.where(c <= r, 1.0, 0.0).astype(BF16)
    lane = lax.broadcasted_iota(I32, (blk, HEAD_DIM), 1)
    carry = jnp.zeros((1, LANES), F32)
    for b in range(S // blk):
        rows = slice(b * blk, (b + 1) * blk)
        hi, mid, lo = _split3(side_ref[rows, :])
        cb = (_dot(tri, hi) + _dot(tri, mid)) + _dot(tri, lo) + carry
        carry = cb[blk - 1:blk, :]
        for h in range(HEADS):
            parts = _split3(cb[:, SIDE_F + h:SIDE_F + h + 1] * (HEAD_DIM ** 0.5))
            ka = jnp.where((lane >= 3) & (lane < 6), 1.0, 0.0)
            qa = jnp.where(lane < 3, 1.0, 0.0)
            for n, part in enumerate(parts):
                ka = jnp.where(lane == n, -part.astype(F32), ka)
                qa = jnp.where(lane == 3 + n, part.astype(F32), qa)
            ka_ref[rows, _head(h)] = ka.astype(BF16)
            qa_ref[rows, _head(h)] = qa.astype(BF16)


def _cum(side3, blk):
    B, S, _ = side3.shape
    spec = pl.BlockSpec((None, S, BRANCH_W), lambda b: (b, 0, 0))
    shape = jax.ShapeDtypeStruct((B, S, BRANCH_W), BF16)
    return pl.pallas_call(
        functools.partial(_cum_kernel, blk=blk),
        grid=(B,),
        in_specs=[pl.BlockSpec((None, S, LANES), lambda b: (b, 0, 0))],
        out_specs=[spec, spec],
        out_shape=[shape, shape],
        compiler_params=_params("arbitrary"),
        name="forget_cumsum",
    )(side3)


def _online_multi(sts, vts, carries):
    stats = []
    for st, (m, l, acc) in zip(sts, carries):
        m_new = jnp.maximum(m, jnp.max(st, axis=0, keepdims=True))
        alpha = jnp.exp2(m - m_new)
        p = jnp.exp2(st - m_new)
        stats.append((m_new, alpha, alpha * l + jnp.sum(p, axis=0, keepdims=True), p.astype(BF16)))
    out = []
    for vt, (m_new, alpha, l, p), (_, _, acc) in zip(vts, stats, carries):
        out.append((m_new, l, alpha * acc + _dot(vt, p)))
    return out


def _flash_loop(i, scores, values, diag_mask, init):
    def body(t, carry):
        return tuple(_online_multi(scores(t), values(t), carry))

    carry = lax.fori_loop(0, i, body, tuple(init))
    return _online_multi([diag_mask(s) for s in scores(i)], values(i), carry)


def _softmax_init_t(cols):
    return (jnp.full((1, cols), NEG, F32), jnp.zeros((1, cols), F32), jnp.zeros((HEAD_DIM, cols), F32))


def _attn_a_kernel(q_ref, k_ref, v_ref, bias_ref, o_ref, *, tq, win):
    i = pl.program_id(1)
    start = pl.multiple_of(jnp.maximum(i * tq - A_LEFT, 0), tq)
    scale = HEAD_DIM ** -0.5
    ss = [_dot_t(q_ref[:, _head(h)], k_ref[pl.ds(start, win), _head(h)]) * scale + bias_ref[h]
          for h in range(HEADS)]
    ps = []
    for s in ss:
        p = jnp.exp(s - jnp.max(s, axis=-1, keepdims=True))
        ps.append((p.astype(BF16), jnp.sum(p, axis=-1, keepdims=True)))
    for h, (p, l) in enumerate(ps):
        o_ref[:, _head(h)] = (_dot(p, v_ref[pl.ds(start, win), _head(h)]) / l).astype(BF16)


def _attn_a(z3, table, tq):
    B, S, _ = z3.shape
    nvar, _, _, win = table.shape
    return pl.pallas_call(
        functools.partial(_attn_a_kernel, tq=tq, win=win),
        grid=(B, S // tq),
        in_specs=[
            pl.BlockSpec((None, tq, TN), lambda b, i: (b, i, T_AQ)),
            pl.BlockSpec((None, S, TN), lambda b, i: (b, 0, T_AK)),
            pl.BlockSpec((None, S, TN), lambda b, i: (b, 0, T_AV)),
            pl.BlockSpec((None, HEADS, tq, win), lambda b, i: (jnp.minimum(i, nvar - 1), 0, 0, 0)),
        ],
        out_specs=pl.BlockSpec((None, tq, BRANCH_W), lambda b, i: (b, i, 0)),
        out_shape=jax.ShapeDtypeStruct((B, S, BRANCH_W), BF16),
        compiler_params=_params("arbitrary", "arbitrary"),
        name="attn_band",
    )(z3, z3, z3, table)


def _band_table(rel_bias_l, tq):
    H = rel_bias_l.shape[0]
    win = A_LEFT + tq
    nvar = A_LEFT // tq + 1
    wfull = win + A_LEFT
    r0 = A_LEFT + tq - 1
    length = wfull + tq - 1
    rb = rel_bias_l.astype(F32)
    g = jnp.concatenate(
        [jnp.broadcast_to(rb[:, 2 * REL_CLIP:], (H, r0 - REL_CLIP)),
         rb[:, ::-1],
         jnp.broadcast_to(rb[:, :1], (H, length - (r0 - REL_CLIP) - (2 * REL_CLIP + 1)))], axis=1)
    gp = jnp.pad(g, ((0, 0), (0, 1)))
    skew = jnp.tile(gp, (1, tq))[:, :tq * length].reshape(H, tq, length)
    full = skew[:, :, tq - 1:tq - 1 + wfull]

    qi = np.arange(tq)[:, None]
    kj = np.arange(win)[None, :]
    tables = []
    for v in range(nvar):
        d = A_LEFT - v * tq
        kk = kj + d
        cq = (A_LEFT + qi) // CHUNK
        ck = kk // CHUNK
        valid = (ck <= cq) & (ck >= cq - A_LEFT_CHUNKS)
        tables.append(jnp.where(valid[None], full[:, :, d:d + win], NEG))
    return jnp.stack(tables)


def _attn_b_kernel(q_ref, k_ref, vt_ref, lam_ref, g_ref, o_ref, *, tq, lambda_init):
    i = pl.program_id(1)
    scale = DIFF_DIM ** -0.5 * LOG2E
    lv = lam_ref[...]
    lam = (jnp.exp(jnp.sum(lv[0:1] * lv[1:2], axis=-1, keepdims=True))
           - jnp.exp(jnp.sum(lv[2:3] * lv[3:4], axis=-1, keepdims=True)) + lambda_init)
    lane = lax.broadcasted_iota(I32, (tq, HEAD_DIM), 1)
    qq = []
    for h in range(HEADS):
        q = q_ref[:, _head(h)]
        zero = jnp.zeros_like(q)
        qq.append(jnp.concatenate([jnp.where(lane < DIFF_DIM, q, zero), jnp.where(lane >= DIFF_DIM, q, zero)], axis=0))
    kchunk = lax.broadcasted_iota(I32, (tq, 2 * tq), 0) >> 6
    qchunk = (lax.broadcasted_iota(I32, (tq, 2 * tq), 1) & (tq - 1)) >> 6
    ok = kchunk <= qchunk

    def scores(t):
        ks = pl.multiple_of(t * tq, tq)
        return [_dot_t(k_ref[pl.ds(ks, tq), _head(h)], qq[h]) * scale for h in range(HEADS)]

    carry = _flash_loop(i, scores, lambda t: [vt_ref[t, _head(h), :] for h in range(HEADS)],
                        lambda s: jnp.where(ok, s, NEG), [_softmax_init_t(2 * tq) for _ in range(HEADS)])
    for h in range(HEADS):
        m, l, acc = carry[h]
        on = acc / l
        o = (on[:, :tq] - lam * on[:, tq:]).T
        y = o * lax.rsqrt(jnp.mean(o * o, axis=-1, keepdims=True) + EPS)
        o_ref[:, _head(h)] = ((y * g_ref[...]) * (1.0 - lambda_init)).astype(BF16)


def _attn_b(z3, vt, lamv, g, tq, lambda_init):
    B, S, _ = z3.shape
    nt = S // tq
    return pl.pallas_call(
        functools.partial(_attn_b_kernel, tq=tq, lambda_init=lambda_init),
        grid=(B, nt),
        in_specs=[
            pl.BlockSpec((None, tq, TN), lambda b, i: (b, i, T_BQ)),
            pl.BlockSpec((None, S, TN), lambda b, i: (b, 0, T_BK)),
            pl.BlockSpec((nt, BRANCH_W, tq), lambda b, i: (b, 0, 0)),
            pl.BlockSpec((4, DIFF_DIM), lambda b, i: (0, 0)),
            pl.BlockSpec((1, HEAD_DIM), lambda b, i: (0, 0)),
        ],
        out_specs=pl.BlockSpec((None, tq, BRANCH_W), lambda b, i: (b, i, 0)),
        out_shape=jax.ShapeDtypeStruct((B, S, BRANCH_W), BF16),
        compiler_params=_params("arbitrary", "arbitrary"),
        name="attn_diff",
    )(z3, z3, vt, lamv, g)


def _attn_c_kernel(q_ref, k_ref, vt_ref, ka_ref, qa_ref, o_ref, *, tq):
    i = pl.program_id(1)
    scale = HEAD_DIM ** -0.5 * LOG2E
    q_aug = [jnp.concatenate([q_ref[:, _head(h)], qa_ref[:, _head(h)]], axis=1) for h in range(HEADS)]
    causal = lax.broadcasted_iota(I32, (tq, tq), 0) <= lax.broadcasted_iota(I32, (tq, tq), 1)

    def scores(t):
        ks = pl.multiple_of(t * tq, tq)
        out = []
        for h in range(HEADS):
            k_aug = jnp.concatenate([k_ref[pl.ds(ks, tq), _head(h)], ka_ref[pl.ds(ks, tq), _head(h)]], axis=1)
            out.append(_dot_t(k_aug, q_aug[h]) * scale)
        return out

    carry = _flash_loop(i, scores, lambda t: [vt_ref[t, _head(h), :] for h in range(HEADS)],
                        lambda s: jnp.where(causal, s, NEG), [_softmax_init_t(tq) for _ in range(HEADS)])
    for h in range(HEADS):
        m, l, acc = carry[h]
        o_ref[:, _head(h)] = (acc / l).T.astype(BF16)


def _attn_c(z3, vt, ka, qa, tq):
    B, S, _ = z3.shape
    nt = S // tq
    return pl.pallas_call(
        functools.partial(_attn_c_kernel, tq=tq),
        grid=(B, nt),
        in_specs=[
            pl.BlockSpec((None, tq, TN), lambda b, i: (b, i, T_CQ)),
            pl.BlockSpec((None, S, TN), lambda b, i: (b, 0, T_CK)),
            pl.BlockSpec((nt, BRANCH_W, tq), lambda b, i: (b, 0, 0)),
            pl.BlockSpec((None, S, BRANCH_W), lambda b, i: (b, 0, 0)),
            pl.BlockSpec((None, tq, BRANCH_W), lambda b, i: (b, i, 0)),
        ],
        out_specs=pl.BlockSpec((None, tq, BRANCH_W), lambda b, i: (b, i, 0)),
        out_shape=jax.ShapeDtypeStruct((B, S, BRANCH_W), BF16),
        compiler_params=_params("arbitrary", "arbitrary"),
        name="attn_forget",
    )(z3, z3, vt, ka, qa)


def _attn_d_kernel(q_ref, iq_ref, kv_ref, vt_ref, sidet_ref, o_ref, sc_ref, *, tq, k_sel):
    i = pl.program_id(1)
    n_t = i + 1
    scale = HEAD_DIM ** -0.5 * LOG2E
    ninf = float("-inf")
    qpos = i * tq + lax.broadcasted_iota(I32, (1, tq), 1)
    k_row = jnp.minimum(k_sel, ((qpos >> 6) + 1) * CHUNK).astype(F32)
    lane = lax.broadcasted_iota(I32, (tq, LANES), 1)
    ok = ((i * tq + lax.broadcasted_iota(I32, (tq, tq), 0)) >> 6) <= ((i * tq + lax.broadcasted_iota(I32, (tq, tq), 1)) >> 6)

    lhs, iws = [], []
    for hi in range(IDX_HEADS):
        blk = iq_ref[:, (hi // 2) * LANES:(hi // 2 + 1) * LANES]
        in_half = (lane >= IDX_DIM) if hi % 2 else (lane < IDX_DIM)
        lhs.append(jnp.where(in_half, blk, jnp.zeros_like(blk)))
        iws.append(sidet_ref[SIDE_IW + hi:SIDE_IW + hi + 1, :])

    def score_tile(t, masked):
        ks = pl.multiple_of(t * tq, tq)
        ik = kv_ref[pl.ds(ks, tq), MISC_IK:MISC_IK + LANES]
        sc = iws[0] * jnp.maximum(_dot_t(ik, lhs[0]), 0.0)
        for hi in range(1, IDX_HEADS):
            sc = sc + iws[hi] * jnp.maximum(_dot_t(ik, lhs[hi]), 0.0)
        if masked:
            sc = jnp.where(ok, sc, ninf)
        sc_ref[t] = sc

    def p1(t, c):
        score_tile(t, False)
        return c

    lax.fori_loop(0, i, p1, 0)
    score_tile(i, True)
    sc_ref[n_t] = jnp.full((tq, tq), ninf, F32)

    def count(pred_fn):
        def body(p, c):
            for t in (2 * p, 2 * p + 1):
                c = c + jnp.sum(jnp.where(pred_fn(sc_ref[t]), 1.0, 0.0), axis=0, keepdims=True)
            return c

        return lax.fori_loop(0, (n_t + 1) >> 1, body, jnp.zeros((1, tq), F32))

    def as_float(u):
        key = u ^ INT_MIN
        return lax.bitcast_convert_type(jnp.where(key < 0, key ^ 0x7FFFFFFF, key), F32)

    def bit_step(it, ans):
        cand = ans | lax.shift_left(jnp.int32(1), 31 - it)
        cf = as_float(cand)
        cnt = count(lambda st: st >= cf)
        return jnp.where(cnt >= k_row, cand, ans)

    thr = as_float(lax.fori_loop(0, 32, bit_step, jnp.zeros((1, tq), I32)))
    cnt_ge = count(lambda st: st >= thr)

    @pl.when(jnp.max(cnt_ge - k_row) > 0.0)
    def _():
        need = k_row - count(lambda st: st > thr)
        r = lax.broadcasted_iota(I32, (tq, tq), 0)
        c = lax.broadcasted_iota(I32, (tq, tq), 1)
        lower = jnp.where(c <= r, 1.0, 0.0).astype(BF16)

        def body(t, run):
            st = sc_ref[t]
            eq = st == thr
            rank = run + _dot(lower, jnp.where(eq, 1.0, 0.0).astype(BF16))
            sc_ref[t] = jnp.where(eq & (rank > need), ninf, st)
            return rank[tq - 1:tq, :]

        lax.fori_loop(0, n_t, body, jnp.zeros((1, tq), F32))

        key_idx = lax.broadcasted_iota(I32, (tq, tq), 0).astype(F32)

        def drop_round(e):
            def smallest(t, v):
                st = sc_ref[t]
                return jnp.minimum(v, jnp.min(jnp.where(st >= thr, st, jnp.inf), axis=0, keepdims=True))

            v = lax.fori_loop(0, n_t, smallest, jnp.full((1, tq), jnp.inf, F32))

            def last(t, j):
                idx = key_idx + (t * tq).astype(F32)
                return jnp.maximum(j, jnp.max(jnp.where(sc_ref[t] == v, idx, -1.0), axis=0, keepdims=True))

            j = lax.fori_loop(0, n_t, last, jnp.full((1, tq), -1.0, F32))

            def kill(t, c):
                idx = key_idx + (t * tq).astype(F32)
                sc_ref[t] = jnp.where((e > 0.0) & (idx == j), ninf, sc_ref[t])
                return c

            lax.fori_loop(0, n_t, kill, 0)
            return e - jnp.where(e > 0.0, 1.0, 0.0)

        lax.while_loop(lambda e: jnp.max(e) > 0.0, drop_round, count(lambda st: st >= thr) - k_row)

    q4 = jnp.concatenate([q_ref[:, _head(h)] for h in range(HEADS)], axis=0)

    def scores(t):
        ks = pl.multiple_of(t * tq, tq)
        s = _dot_t(kv_ref[pl.ds(ks, tq), MISC_DK:MISC_DK + LANES], q4) * scale
        sel = sc_ref[t] >= thr
        return [jnp.concatenate([jnp.where(sel, s[:, h * tq:(h + 1) * tq], NEG) for h in range(HEADS)], axis=1)]

    (m, l, acc), = _flash_loop(i, scores, lambda t: [vt_ref[t]], lambda s: s, [_softmax_init_t(HEADS * tq)])
    o = acc / l
    for h in range(HEADS):
        o_ref[:, _head(h)] = o[:, h * tq:(h + 1) * tq].T.astype(BF16)


def _attn_d(z3, vt, sidet, tq, k_sel):
    B, S, _ = z3.shape
    nt = S // tq
    return pl.pallas_call(
        functools.partial(_attn_d_kernel, tq=tq, k_sel=k_sel),
        grid=(B, nt),
        in_specs=[
            pl.BlockSpec((None, tq, TN), lambda b, i: (b, i, T_DQ)),
            pl.BlockSpec((None, tq, TN), lambda b, i: (b, i, T_DIQ)),
            pl.BlockSpec((None, S, TN), lambda b, i: (b, 0, T_MISC)),
            pl.BlockSpec((nt, HEAD_DIM, tq), lambda b, i: (b, 0, 0)),
            pl.BlockSpec((None, LANES, tq), lambda b, i: (b * nt + i, 0, 0)),
        ],
        out_specs=pl.BlockSpec((None, tq, BRANCH_W), lambda b, i: (b, i, 0)),
        out_shape=jax.ShapeDtypeStruct((B, S, BRANCH_W), BF16),
        scratch_shapes=[pltpu.VMEM((nt + 1, tq, tq), F32)],
        compiler_params=_params("arbitrary", "arbitrary"),
        name="attn_select",
    )(z3, z3, z3, vt, sidet)


def _merge_kernel(oa_ref, ob_ref, oc_ref, od_ref, wb_ref, ga_ref, gb_ref, gc_ref, gd_ref, m_ref):
    acc = None
    for idx, (o_ref, g_ref) in enumerate(((oa_ref, ga_ref), (ob_ref, gb_ref), (oc_ref, gc_ref), (od_ref, gd_ref))):
        term = g_ref[...].astype(F32) * _dot(o_ref[...], wb_ref[idx])
        acc = term if acc is None else acc + term
    m_ref[...] = acc.astype(BF16)


def _merge(outs, wb, z, layer, tm):
    T = z.shape[0]
    D = wb.shape[-1]
    nn = D // TN
    o_spec = pl.BlockSpec((tm, BRANCH_W), lambda i, n: (i, 0))
    g_specs = [pl.BlockSpec((tm, TN), functools.partial(lambda i, n, br: (i, T_GATE + br * nn + n), br=br))
               for br in range(N_BRANCH)]
    return pl.pallas_call(
        _merge_kernel,
        grid=(T // tm, nn),
        in_specs=[o_spec] * 4 + [pl.BlockSpec((None, N_BRANCH, BRANCH_W, TN), lambda i, n: (layer, 0, 0, n))] + g_specs,
        out_specs=pl.BlockSpec((tm, TN), lambda i, n: (i, n)),
        out_shape=jax.ShapeDtypeStruct((T, D), BF16),
        compiler_params=_params("arbitrary", "arbitrary"),
        name="gated_merge",
    )(*outs, wb, z, z, z, z)


def _out_proj_kernel(m_ref, w_ref, x_ref, o_ref):
    o_ref[...] = x_ref[...] + _dot(m_ref[...], w_ref[...])


def _out_proj(merged, w, x2, layer, tm):
    T, D = x2.shape
    return pl.pallas_call(
        _out_proj_kernel,
        grid=(T // tm, D // TN),
        in_specs=[
            pl.BlockSpec((tm, D), lambda i, n: (i, 0)),
            pl.BlockSpec((None, D, TN), lambda i, n: (layer, 0, n)),
            pl.BlockSpec((tm, TN), lambda i, n: (i, n)),
        ],
        out_specs=pl.BlockSpec((tm, TN), lambda i, n: (i, n)),
        out_shape=jax.ShapeDtypeStruct((T, D), F32),
        compiler_params=_params("arbitrary", "arbitrary"),
        name="out_proj",
    )(merged, w, x2)


def _ffn_kernel(x_ref, g_ref, w1_ref, w2_ref, fg_ref, o_ref, h_ref, *, final_norm):
    f = pl.program_id(1)

    @pl.when(f == 0)
    def _():
        x = x_ref[...]
        y = x * lax.rsqrt(jnp.mean(x * x, axis=-1, keepdims=True) + EPS)
        h_ref[...] = (y * g_ref[...]).astype(BF16)
        o_ref[...] = x

    u = jnp.maximum(_dot(h_ref[...], w1_ref[...]), 0.0)
    o_ref[...] += _dot((u * u).astype(BF16), w2_ref[...])

    if final_norm:
        @pl.when(f == pl.num_programs(1) - 1)
        def _():
            y = o_ref[...]
            y = y * lax.rsqrt(jnp.mean(y * y, axis=-1, keepdims=True) + EPS)
            o_ref[...] = y * fg_ref[...]


def _ffn(x2, g, w1, w2, fg, layer, tm, tf, final_norm):
    T, D = x2.shape
    dff = w1.shape[-1]
    return pl.pallas_call(
        functools.partial(_ffn_kernel, final_norm=final_norm),
        grid=(T // tm, dff // tf),
        in_specs=[
            pl.BlockSpec((tm, D), lambda i, f: (i, 0)),
            pl.BlockSpec((1, D), lambda i, f: (0, 0)),
            pl.BlockSpec((None, D, tf), lambda i, f: (layer, 0, f)),
            pl.BlockSpec((None, tf, D), lambda i, f: (layer, f, 0)),
            pl.BlockSpec((1, D), lambda i, f: (0, 0)),
        ],
        out_specs=pl.BlockSpec((tm, D), lambda i, f: (i, 0)),
        out_shape=jax.ShapeDtypeStruct((T, D), F32),
        scratch_shapes=[pltpu.VMEM((tm, D), BF16)],
        compiler_params=_params("arbitrary", "arbitrary"),
        name="ffn",
    )(x2, g, w1, w2, fg)


def _w_in_offsets(n_in):
    off = {}
    o = 0
    for name, width in (("abc", 9 * BRANCH_W), ("c_f", HEADS), ("d_q", BRANCH_W), ("d_k", HEAD_DIM),
                        ("d_v", HEAD_DIM), ("d_iq", IDX_HEADS * IDX_DIM), ("d_ik", IDX_DIM),
                        ("d_iw", IDX_HEADS), ("gate", None)):
        width = n_in - o if width is None else width
        off[name] = (o, o + width)
        o += width
    return off


def _pack_w_in(w_in):
    L, D, n_in = w_in.shape
    off = _w_in_offsets(n_in)
    wt = jnp.transpose(w_in, (2, 0, 1)).astype(BF16)

    def seg(name):
        a, b = off[name]
        return wt[a:b]

    pad = jnp.zeros((LANES - HEADS - IDX_HEADS, L, D), BF16)
    packed = jnp.concatenate(
        [seg("abc"), seg("d_q"), seg("d_iq"),
         seg("d_k"), seg("d_v"), seg("d_ik"), seg("d_ik"), seg("c_f"), seg("d_iw"), pad,
         seg("gate")], axis=0)
    return packed.reshape(N_TILES * TN, L * D)


def _rope_table(seq):
    pos = jnp.arange(seq, dtype=F32)[:, None]

    def cs(dim):
        inv = ROPE_THETA ** (-jnp.arange(0, dim, 2, dtype=F32) / dim)
        ang = pos * inv[None, :]
        return jnp.cos(ang), jnp.sin(ang)

    c64, s64 = cs(DIFF_DIM)
    c128, s128 = cs(HEAD_DIM)
    return jnp.concatenate(
        [jnp.tile(c64, (1, 4)), jnp.tile(jnp.concatenate([-s64, s64], axis=1), (1, 2)),
         jnp.tile(c128, (1, 2)), jnp.concatenate([-s128, s128], axis=1)], axis=1)


def kernel(x, norm1_g, norm2_g, final_g, w_in, b_gate, b_forget, rel_bias, lambda_q1, lambda_k1,
           lambda_q2, lambda_k2, diff_norm_g, w_branch, w_out, w_ff1, w_ff2):
    B, S, D = x.shape
    depth = w_in.shape[0]
    T = B * S
    k_sel = min(TOPK_MAX, S // 4)
    tm = min(1024, S)
    tm_ffn = min(1024, S)
    tf = 512
    tq_a = 128
    tq = 256

    w_in_p = _pack_w_in(w_in)
    wb = w_branch.astype(BF16)
    wo = w_out.astype(BF16)
    w1 = w_ff1.astype(BF16)
    w2 = w_ff2.astype(BF16)
    rope = _rope_table(S)
    zeros_pre = jnp.zeros((depth, T_GATE * TN), F32)
    bias_in = jnp.concatenate([zeros_pre, b_gate.astype(F32)], axis=1)[:, None, :]
    bf = jnp.pad(b_forget.astype(F32), ((0, 0), (SIDE_F, LANES - SIDE_F - HEADS)))
    lamv = jnp.stack([lambda_q1, lambda_k1, lambda_q2, lambda_k2], axis=1).astype(F32)

    x2 = x.reshape(T, D)
    for l in range(depth):
        z, side, sidet, vtb, vtc, vtd = _in_proj(
            x2, norm1_g[l][None].astype(F32), w_in_p, bias_in, rope, bf[l][None], l, S, tm, tq)
        z3 = z.reshape(B, S, N_TILES * TN)
        ka, qa = _cum(side.reshape(B, S, LANES), tq)
        lambda_init = 0.8 - 0.6 * math.exp(-0.3 * l)
        oa = _attn_a(z3, _band_table(rel_bias[l], tq_a), tq_a)
        ob = _attn_b(z3, vtb, lamv[l], diff_norm_g[l][None].astype(F32), tq, lambda_init)
        oc = _attn_c(z3, vtc, ka, qa, tq)
        od = _attn_d(z3, vtd, sidet, tq, k_sel)
        outs = [o.reshape(T, BRANCH_W) for o in (oa, ob, oc, od)]
        merged = _merge(outs, wb, z, l, tm)
        x2 = _out_proj(merged, wo, x2, l, tm)
        x2 = _ffn(x2, norm2_g[l][None].astype(F32), w1, w2, final_g[None].astype(F32),
                  l, tm_ffn, tf, final_norm=(l == depth - 1))
    return x2.reshape(B, S, D)
```

```python
import functools
import math

import numpy as np
import jax
import jax.numpy as jnp
from jax import lax
from jax.experimental import pallas as pl
from jax.experimental.pallas import tpu as pltpu

F32 = jnp.float32
BF16 = jnp.bfloat16
I32 = jnp.int32

CHUNK = 64
N_BRANCH = 4
HEAD_DIM = 128
HEADS = 4
BRANCH_W = HEADS * HEAD_DIM
A_LEFT_CHUNKS = 8
A_LEFT = A_LEFT_CHUNKS * CHUNK
REL_CLIP = 128
DIFF_DIM = HEAD_DIM // 2
IDX_HEADS = 8
IDX_DIM = 64
TOPK_MAX = 256
ROPE_THETA = 10000.0
EPS = 1e-6

LANES = 128
LOG2E = 1.4426950408889634
NEG = -1e30
INT_MIN = -2147483648
VMEM_LIMIT = 56 * 1024 * 1024

TN = 512
T_AQ, T_AK, T_AV, T_BQ, T_BK, T_BV, T_CQ, T_CK, T_CV, T_DQ, T_DIQ, T_MISC, T_GATE = range(13)
N_TILES = T_GATE + N_BRANCH * 4
MISC_DK, MISC_DV, MISC_IK, MISC_SIDE = 0, 128, 256, 384
SIDE_F, SIDE_IW = 0, 4


def _dot(a, b):
    return jnp.dot(a, b, preferred_element_type=F32)


def _dot_t(a, b):
    return lax.dot_general(a, b, (((1,), (1,)), ((), ())), preferred_element_type=F32)


def _params(*sem):
    return pltpu.CompilerParams(dimension_semantics=sem, vmem_limit_bytes=VMEM_LIMIT)


def _head(h):
    return slice(h * HEAD_DIM, (h + 1) * HEAD_DIM)


def _rope_half(blk, cos, sin_signed, half):
    if 2 * half == LANES:
        rot = pltpu.roll(blk, half, axis=1)
    else:
        lane = lax.broadcasted_iota(I32, blk.shape, 1)
        first = (lane & (2 * half - 1)) < half
        rot = jnp.where(first, pltpu.roll(blk, LANES - half, axis=1), pltpu.roll(blk, half, axis=1))
    return blk * cos + rot * sin_signed


def _in_proj_kernel(x_ref, g_ref, w_ref, bias_ref, rope_ref, bf_ref,
                    z_ref, side_ref, sidet_ref, vtb_ref, vtc_ref, vtd_ref, h_ref, *, tk):
    j = pl.program_id(1)
    n_sub = x_ref.shape[0] // tk

    @pl.when(j == 0)
    def _():
        x = x_ref[...]
        y = x * lax.rsqrt(jnp.mean(x * x, axis=-1, keepdims=True) + EPS)
        h_ref[...] = (y * g_ref[...]).astype(BF16)

    acc = _dot_t(h_ref[...], w_ref[...])

    def cols(a, c):
        return a[:, c * LANES:(c + 1) * LANES]

    def rope64(blk):
        return _rope_half(blk, rope_ref[:, 0:128], rope_ref[:, 128:256], 32)

    def rope128(blk):
        return _rope_half(blk, rope_ref[:, 256:384], rope_ref[:, 384:512], 64)

    def store_t(dst_ref, a):
        for c in range(n_sub):
            dst_ref[c] = a[c * tk:(c + 1) * tk, :].T.astype(dst_ref.dtype)

    is_plain = ((j <= T_AV) | (j == T_CQ) | (j == T_CK))

    @pl.when(is_plain)
    def _():
        z_ref[...] = acc.astype(BF16)

    @pl.when(j == T_BV)
    def _():
        z_ref[...] = acc.astype(BF16)
        store_t(vtb_ref, acc)

    @pl.when(j == T_CV)
    def _():
        z_ref[...] = acc.astype(BF16)
        store_t(vtc_ref, acc)

    @pl.when((j == T_BQ) | (j == T_BK))
    def _():
        for c in range(4):
            z_ref[:, c * LANES:(c + 1) * LANES] = rope64(cols(acc, c)).astype(BF16)

    @pl.when(j == T_DQ)
    def _():
        for c in range(4):
            z_ref[:, c * LANES:(c + 1) * LANES] = rope128(cols(acc, c)).astype(BF16)

    @pl.when(j == T_DIQ)
    def _():
        for c in range(4):
            z_ref[:, c * LANES:(c + 1) * LANES] = (rope64(cols(acc, c)) * (IDX_DIM ** -0.5)).astype(BF16)

    @pl.when(j == T_MISC)
    def _():
        z_ref[:, MISC_DK:MISC_DK + LANES] = rope128(cols(acc, 0)).astype(BF16)
        dv = cols(acc, 1)
        z_ref[:, MISC_DV:MISC_DV + LANES] = dv.astype(BF16)
        store_t(vtd_ref, dv)
        z_ref[:, MISC_IK:MISC_IK + LANES] = rope64(cols(acc, 2)).astype(BF16)
        raw = cols(acc, 3)
        z_ref[:, MISC_SIDE:MISC_SIDE + LANES] = raw.astype(BF16)
        lane = lax.broadcasted_iota(I32, raw.shape, 1)
        xf = raw + bf_ref[...]
        log_f = jnp.minimum(xf, 0.0) - jnp.log(1.0 + jnp.exp(-jnp.abs(xf)))
        iw = raw * (IDX_HEADS ** -0.5)
        side = jnp.where(lane < SIDE_IW, log_f, jnp.where(lane < SIDE_IW + IDX_HEADS, iw, 0.0))
        side_ref[...] = side
        store_t(sidet_ref, side)

    @pl.when(j >= T_GATE)
    def _():
        z_ref[...] = (1.0 / (1.0 + jnp.exp(-(acc + bias_ref[...])))).astype(BF16)


def _in_proj(x2, g, w, bias, rope, bf, layer, seq, tm, tk):
    T, D = x2.shape
    n_pos = seq // tm
    n_sub = tm // tk

    def t_spec(rows):
        return pl.BlockSpec((n_sub, rows, tk), lambda i, j: (i, 0, 0))

    return pl.pallas_call(
        functools.partial(_in_proj_kernel, tk=tk),
        grid=(T // tm, N_TILES),
        in_specs=[
            pl.BlockSpec((tm, D), lambda i, j: (i, 0)),
            pl.BlockSpec((1, D), lambda i, j: (0, 0)),
            pl.BlockSpec((TN, D), lambda i, j: (j, layer)),
            pl.BlockSpec((None, 1, TN), lambda i, j: (layer, 0, j)),
            pl.BlockSpec((tm, TN), lambda i, j: (i % n_pos, 0)),
            pl.BlockSpec((1, LANES), lambda i, j: (0, 0)),
        ],
        out_specs=[
            pl.BlockSpec((tm, TN), lambda i, j: (i, j)),
            pl.BlockSpec((tm, LANES), lambda i, j: (i, 0)),
            t_spec(LANES), t_spec(BRANCH_W), t_spec(BRANCH_W), t_spec(HEAD_DIM),
        ],
        out_shape=[
            jax.ShapeDtypeStruct((T, N_TILES * TN), BF16),
            jax.ShapeDtypeStruct((T, LANES), F32),
            jax.ShapeDtypeStruct((T // tk, LANES, tk), F32),
            jax.ShapeDtypeStruct((T // tk, BRANCH_W, tk), BF16),
            jax.ShapeDtypeStruct((T // tk, BRANCH_W, tk), BF16),
            jax.ShapeDtypeStruct((T // tk, HEAD_DIM, tk), BF16),
        ],
        scratch_shapes=[pltpu.VMEM((tm, D), BF16)],
        compiler_params=_params("arbitrary", "arbitrary"),
        name="in_proj",
    )(x2, g, w, bias, rope, bf)


def _split3(x):
    hi = x.astype(BF16)
    r1 = x - hi.astype(F32)
    mid = r1.astype(BF16)
    lo = (r1 - mid.astype(F32)).astype(BF16)
    return hi, mid, lo


def _cum_kernel(side_ref, ka_ref, qa_ref, *, blk):
    S = side_ref.shape[0]
    r = lax.broadcasted_iota(I32, (blk, blk), 0)
    c = lax.broadcasted_iota(I32, (blk, blk), 1)
    tri = jnp.where(c <= r, 1.0, 0.0).astype(BF16)
    lane = lax.broadcasted_iota(I32, (blk, HEAD_DIM), 1)
    carry = jnp.zeros((1, LANES), F32)
    for b in range(S // blk):
        rows = slice(b * blk, (b + 1) * blk)
        hi, mid, lo = _split3(side_ref[rows, :])
        cb = (_dot(tri, hi) + _dot(tri, mid)) + _dot(tri, lo) + carry
        carry = cb[blk - 1:blk, :]
        for h in range(HEADS):
            parts = _split3(cb[:, SIDE_F + h:SIDE_F + h + 1] * (HEAD_DIM ** 0.5))
            ka = jnp.where((lane >= 3) & (lane < 6), 1.0, 0.0)
            qa = jnp.where(lane < 3, 1.0, 0.0)
            for n, part in enumerate(parts):
                ka = jnp.where(lane == n, -part.astype(F32), ka)
                qa = jnp.where(lane == 3 + n, part.astype(F32), qa)
            ka_ref[rows, _head(h)] = ka.astype(BF16)
            qa_ref[rows, _head(h)] = qa.astype(BF16)


def _cum(side3, blk):
    B, S, _ = side3.shape
    spec = pl.BlockSpec((None, S, BRANCH_W), lambda b: (b, 0, 0))
    shape = jax.ShapeDtypeStruct((B, S, BRANCH_W), BF16)
    return pl.pallas_call(
        functools.partial(_cum_kernel, blk=blk),
        grid=(B,),
        in_specs=[pl.BlockSpec((None, S, LANES), lambda b: (b, 0, 0))],
        out_specs=[spec, spec],
        out_shape=[shape, shape],
        compiler_params=_params("arbitrary"),
        name="forget_cumsum",
    )(side3)


def _online_multi(sts, vts, carries):
    stats = []
    for st, (m, l, acc) in zip(sts, carries):
        m_new = jnp.maximum(m, jnp.max(st, axis=0, keepdims=True))
        alpha = jnp.exp2(m - m_new)
        p = jnp.exp2(st - m_new)
        stats.append((m_new, alpha, alpha * l + jnp.sum(p, axis=0, keepdims=True), p.astype(BF16)))
    out = []
    for vt, (m_new, alpha, l, p), (_, _, acc) in zip(vts, stats, carries):
        out.append((m_new, l, alpha * acc + _dot(vt, p)))
    return out


def _flash_loop(i, scores, values, tail_mask, init):
    def pair(ta, carry, masked):
        sa, sb = scores(ta), scores(ta + 1)
        if masked:
            sa = [tail_mask(s, ta) for s in sa]
            sb = [tail_mask(s, ta + 1) for s in sb]
        carry = _online_multi(sa, values(ta), carry)
        return tuple(_online_multi(sb, values(ta + 1), carry))

    half = lax.shift_right_logical(i, 1)
    carry = lax.fori_loop(0, half, lambda p, c: pair(2 * p, c, False), tuple(init))
    return pair(2 * half, carry, True)


def _softmax_init_t(cols):
    return (jnp.full((1, cols), NEG, F32), jnp.zeros((1, cols), F32), jnp.zeros((HEAD_DIM, cols), F32))


def _attn_a_kernel(q_ref, k_ref, v_ref, bias_ref, o_ref, *, tq, win):
    i = pl.program_id(1)
    start = pl.multiple_of(jnp.maximum(i * tq - A_LEFT, 0), tq)
    scale = HEAD_DIM ** -0.5
    ss = [_dot_t(q_ref[:, _head(h)], k_ref[pl.ds(start, win), _head(h)]) * scale + bias_ref[h]
          for h in range(HEADS)]
    ps = []
    for s in ss:
        p = jnp.exp(s - jnp.max(s, axis=-1, keepdims=True))
        ps.append((p.astype(BF16), jnp.sum(p, axis=-1, keepdims=True)))
    for h, (p, l) in enumerate(ps):
        o_ref[:, _head(h)] = (_dot(p, v_ref[pl.ds(start, win), _head(h)]) / l).astype(BF16)


def _attn_a(z3, table, tq):
    B, S, _ = z3.shape
    nvar, _, _, win = table.shape
    return pl.pallas_call(
        functools.partial(_attn_a_kernel, tq=tq, win=win),
        grid=(B, S // tq),
        in_specs=[
            pl.BlockSpec((None, tq, TN), lambda b, i: (b, i, T_AQ)),
            pl.BlockSpec((None, S, TN), lambda b, i: (b, 0, T_AK)),
            pl.BlockSpec((None, S, TN), lambda b, i: (b, 0, T_AV)),
            pl.BlockSpec((None, HEADS, tq, win), lambda b, i: (jnp.minimum(i, nvar - 1), 0, 0, 0)),
        ],
        out_specs=pl.BlockSpec((None, tq, BRANCH_W), lambda b, i: (b, i, 0)),
        out_shape=jax.ShapeDtypeStruct((B, S, BRANCH_W), BF16),
        compiler_params=_params("arbitrary", "arbitrary"),
        name="attn_band",
    )(z3, z3, z3, table)


def _band_table(rel_bias_l, tq):
    H = rel_bias_l.shape[0]
    win = A_LEFT + tq
    nvar = A_LEFT // tq + 1
    wfull = win + A_LEFT
    r0 = A_LEFT + tq - 1
    length = wfull + tq - 1
    rb = rel_bias_l.astype(F32)
    g = jnp.concatenate(
        [jnp.broadcast_to(rb[:, 2 * REL_CLIP:], (H, r0 - REL_CLIP)),
         rb[:, ::-1],
         jnp.broadcast_to(rb[:, :1], (H, length - (r0 - REL_CLIP) - (2 * REL_CLIP + 1)))], axis=1)
    gp = jnp.pad(g, ((0, 0), (0, 1)))
    skew = jnp.tile(gp, (1, tq))[:, :tq * length].reshape(H, tq, length)
    full = skew[:, :, tq - 1:tq - 1 + wfull]

    qi = np.arange(tq)[:, None]
    kj = np.arange(win)[None, :]
    tables = []
    for v in range(nvar):
        d = A_LEFT - v * tq
        kk = kj + d
        cq = (A_LEFT + qi) // CHUNK
        ck = kk // CHUNK
        valid = (ck <= cq) & (ck >= cq - A_LEFT_CHUNKS)
        tables.append(jnp.where(valid[None], full[:, :, d:d + win], NEG))
    return jnp.stack(tables)


def _attn_b_kernel(q_ref, k_ref, vt_ref, lam_ref, g_ref, o_ref, *, tq, lambda_init):
    i = pl.program_id(1)
    scale = DIFF_DIM ** -0.5 * LOG2E
    lv = lam_ref[...]
    lam = (jnp.exp(jnp.sum(lv[0:1] * lv[1:2], axis=-1, keepdims=True))
           - jnp.exp(jnp.sum(lv[2:3] * lv[3:4], axis=-1, keepdims=True)) + lambda_init)
    lane = lax.broadcasted_iota(I32, (tq, HEAD_DIM), 1)
    qq = []
    for h in range(HEADS):
        q = q_ref[:, _head(h)]
        zero = jnp.zeros_like(q)
        qq.append(jnp.concatenate([jnp.where(lane < DIFF_DIM, q, zero), jnp.where(lane >= DIFF_DIM, q, zero)], axis=0))
    kchunk = lax.broadcasted_iota(I32, (tq, 2 * tq), 0) >> 6
    qchunk = (lax.broadcasted_iota(I32, (tq, 2 * tq), 1) & (tq - 1)) >> 6
    chunk_lead = kchunk - qchunk
    last = vt_ref.shape[0] - 1

    def scores(t):
        ks = pl.multiple_of(jnp.minimum(t, last) * tq, tq)
        return [_dot_t(k_ref[pl.ds(ks, tq), _head(h)], qq[h]) * scale for h in range(HEADS)]

    def values(t):
        return [vt_ref[jnp.minimum(t, last), _head(h), :] for h in range(HEADS)]

    def tail_mask(s, t):
        return jnp.where(chunk_lead <= (i - t) * (tq // CHUNK), s, NEG)

    carry = _flash_loop(i, scores, values, tail_mask, [_softmax_init_t(2 * tq) for _ in range(HEADS)])
    for h in range(HEADS):
        m, l, acc = carry[h]
        on = acc / l
        o = (on[:, :tq] - lam * on[:, tq:]).T
        y = o * lax.rsqrt(jnp.mean(o * o, axis=-1, keepdims=True) + EPS)
        o_ref[:, _head(h)] = ((y * g_ref[...]) * (1.0 - lambda_init)).astype(BF16)


def _attn_b(z3, vt, lamv, g, tq, lambda_init):
    B, S, _ = z3.shape
    nt = S // tq
    return pl.pallas_call(
        functools.partial(_attn_b_kernel, tq=tq, lambda_init=lambda_init),
        grid=(B, nt),
        in_specs=[
            pl.BlockSpec((None, tq, TN), lambda b, i: (b, i, T_BQ)),
            pl.BlockSpec((None, S, TN), lambda b, i: (b, 0, T_BK)),
            pl.BlockSpec((nt, BRANCH_W, tq), lambda b, i: (b, 0, 0)),
            pl.BlockSpec((4, DIFF_DIM), lambda b, i: (0, 0)),
            pl.BlockSpec((1, HEAD_DIM), lambda b, i: (0, 0)),
        ],
        out_specs=pl.BlockSpec((None, tq, BRANCH_W), lambda b, i: (b, i, 0)),
        out_shape=jax.ShapeDtypeStruct((B, S, BRANCH_W), BF16),
        compiler_params=_params("arbitrary", "arbitrary"),
        name="attn_diff",
    )(z3, z3, vt, lamv, g)


def _attn_c_kernel(q_ref, k_ref, vt_ref, ka_ref, qa_ref, o_ref, *, tq):
    i = pl.program_id(1)
    scale = HEAD_DIM ** -0.5 * LOG2E
    q_aug = [jnp.concatenate([q_ref[:, _head(h)], qa_ref[:, _head(h)]], axis=1) for h in range(HEADS)]
    key_lead = lax.broadcasted_iota(I32, (tq, tq), 0) - lax.broadcasted_iota(I32, (tq, tq), 1)
    last = vt_ref.shape[0] - 1

    def scores(t):
        ks = pl.multiple_of(jnp.minimum(t, last) * tq, tq)
        out = []
        for h in range(HEADS):
            k_aug = jnp.concatenate([k_ref[pl.ds(ks, tq), _head(h)], ka_ref[pl.ds(ks, tq), _head(h)]], axis=1)
            out.append(_dot_t(k_aug, q_aug[h]) * scale)
        return out

    def values(t):
        return [vt_ref[jnp.minimum(t, last), _head(h), :] for h in range(HEADS)]

    def tail_mask(s, t):
        return jnp.where(key_lead <= (i - t) * tq, s, NEG)

    carry = _flash_loop(i, scores, values, tail_mask, [_softmax_init_t(tq) for _ in range(HEADS)])
    for h in range(HEADS):
        m, l, acc = carry[h]
        o_ref[:, _head(h)] = (acc / l).T.astype(BF16)


def _attn_c(z3, vt, ka, qa, tq):
    B, S, _ = z3.shape
    nt = S // tq
    return pl.pallas_call(
        functools.partial(_attn_c_kernel, tq=tq),
        grid=(B, nt),
        in_specs=[
            pl.BlockSpec((None, tq, TN), lambda b, i: (b, i, T_CQ)),
            pl.BlockSpec((None, S, TN), lambda b, i: (b, 0, T_CK)),
            pl.BlockSpec((nt, BRANCH_W, tq), lambda b, i: (b, 0, 0)),
            pl.BlockSpec((None, S, BRANCH_W), lambda b, i: (b, 0, 0)),
            pl.BlockSpec((None, tq, BRANCH_W), lambda b, i: (b, i, 0)),
        ],
        out_specs=pl.BlockSpec((None, tq, BRANCH_W), lambda b, i: (b, i, 0)),
        out_shape=jax.ShapeDtypeStruct((B, S, BRANCH_W), BF16),
        compiler_params=_params("arbitrary", "arbitrary"),
        name="attn_forget",
    )(z3, z3, vt, ka, qa)


def _attn_d_kernel(q_ref, iq_ref, kv_ref, vt_ref, sidet_ref, o_ref, sc_ref, *, tq, k_sel):
    i = pl.program_id(1)
    n_t = i + 1
    scale = HEAD_DIM ** -0.5 * LOG2E
    ninf = float("-inf")
    qpos = i * tq + lax.broadcasted_iota(I32, (1, tq), 1)
    k_row = jnp.minimum(k_sel, ((qpos >> 6) + 1) * CHUNK).astype(F32)
    lane = lax.broadcasted_iota(I32, (tq, LANES), 1)
    ok = ((i * tq + lax.broadcasted_iota(I32, (tq, tq), 0)) >> 6) <= ((i * tq + lax.broadcasted_iota(I32, (tq, tq), 1)) >> 6)

    lhs, iws = [], []
    for hi in range(IDX_HEADS):
        blk = iq_ref[:, (hi // 2) * LANES:(hi // 2 + 1) * LANES]
        in_half = (lane >= IDX_DIM) if hi % 2 else (lane < IDX_DIM)
        lhs.append(jnp.where(in_half, blk, jnp.zeros_like(blk)))
        iws.append(sidet_ref[SIDE_IW + hi:SIDE_IW + hi + 1, :])

    def score_tile(t, masked):
        ks = pl.multiple_of(t * tq, tq)
        ik = kv_ref[pl.ds(ks, tq), MISC_IK:MISC_IK + LANES]
        sc = iws[0] * jnp.maximum(_dot_t(ik, lhs[0]), 0.0)
        for hi in range(1, IDX_HEADS):
            sc = sc + iws[hi] * jnp.maximum(_dot_t(ik, lhs[hi]), 0.0)
        if masked:
            sc = jnp.where(ok, sc, ninf)
        sc_ref[t] = sc

    def p1(t, c):
        score_tile(t, False)
        return c

    lax.fori_loop(0, i, p1, 0)
    score_tile(i, True)
    sc_ref[n_t] = jnp.full((tq, tq), ninf, F32)

    def count(pred_fn):
        def body(p, c):
            for t in (2 * p, 2 * p + 1):
                c = c + jnp.sum(jnp.where(pred_fn(sc_ref[t]), 1.0, 0.0), axis=0, keepdims=True)
            return c

        return lax.fori_loop(0, (n_t + 1) >> 1, body, jnp.zeros((1, tq), F32))

    def as_float(u):
        key = u ^ INT_MIN
        return lax.bitcast_convert_type(jnp.where(key < 0, key ^ 0x7FFFFFFF, key), F32)

    def bit_step(it, ans):
        cand = ans | lax.shift_left(jnp.int32(1), 31 - it)
        cf = as_float(cand)
        cnt = count(lambda st: st >= cf)
        return jnp.where(cnt >= k_row, cand, ans)

    thr = as_float(lax.fori_loop(0, 32, bit_step, jnp.zeros((1, tq), I32)))
    cnt_ge = count(lambda st: st >= thr)

    @pl.when(jnp.max(cnt_ge - k_row) > 0.0)
    def _():
        need = k_row - count(lambda st: st > thr)
        r = lax.broadcasted_iota(I32, (tq, tq), 0)
        c = lax.broadcasted_iota(I32, (tq, tq), 1)
        lower = jnp.where(c <= r, 1.0, 0.0).astype(BF16)

        def body(t, run):
            st = sc_ref[t]
            eq = st == thr
            rank = run + _dot(lower, jnp.where(eq, 1.0, 0.0).astype(BF16))
            sc_ref[t] = jnp.where(eq & (rank > need), ninf, st)
            return rank[tq - 1:tq, :]

        lax.fori_loop(0, n_t, body, jnp.zeros((1, tq), F32))

        key_idx = lax.broadcasted_iota(I32, (tq, tq), 0).astype(F32)

        def drop_round(e):
            def smallest(t, v):
                st = sc_ref[t]
                return jnp.minimum(v, jnp.min(jnp.where(st >= thr, st, jnp.inf), axis=0, keepdims=True))

            v = lax.fori_loop(0, n_t, smallest, jnp.full((1, tq), jnp.inf, F32))

            def last(t, j):
                idx = key_idx + (t * tq).astype(F32)
                return jnp.maximum(j, jnp.max(jnp.where(sc_ref[t] == v, idx, -1.0), axis=0, keepdims=True))

            j = lax.fori_loop(0, n_t, last, jnp.full((1, tq), -1.0, F32))

            def kill(t, c):
                idx = key_idx + (t * tq).astype(F32)
                sc_ref[t] = jnp.where((e > 0.0) & (idx == j), ninf, sc_ref[t])
                return c

            lax.fori_loop(0, n_t, kill, 0)
            return e - jnp.where(e > 0.0, 1.0, 0.0)

        lax.while_loop(lambda e: jnp.max(e) > 0.0, drop_round, count(lambda st: st >= thr) - k_row)

    q4 = jnp.concatenate([q_ref[:, _head(h)] for h in range(HEADS)], axis=0)

    last = vt_ref.shape[0] - 1

    def scores(t):
        ks = pl.multiple_of(jnp.minimum(t, last) * tq, tq)
        s = _dot_t(kv_ref[pl.ds(ks, tq), MISC_DK:MISC_DK + LANES], q4) * scale
        sel = sc_ref[t] >= thr
        return [jnp.concatenate([jnp.where(sel, s[:, h * tq:(h + 1) * tq], NEG) for h in range(HEADS)], axis=1)]

    (m, l, acc), = _flash_loop(i, scores, lambda t: [vt_ref[jnp.minimum(t, last)]], lambda s, t: s,
                               [_softmax_init_t(HEADS * tq)])
    o = acc / l
    for h in range(HEADS):
        o_ref[:, _head(h)] = o[:, h * tq:(h + 1) * tq].T.astype(BF16)


def _attn_d(z3, vt, sidet, tq, k_sel):
    B, S, _ = z3.shape
    nt = S // tq
    return pl.pallas_call(
        functools.partial(_attn_d_kernel, tq=tq, k_sel=k_sel),
        grid=(B, nt),
        in_specs=[
            pl.BlockSpec((None, tq, TN), lambda b, i: (b, i, T_DQ)),
            pl.BlockSpec((None, tq, TN), lambda b, i: (b, i, T_DIQ)),
            pl.BlockSpec((None, S, TN), lambda b, i: (b, 0, T_MISC)),
            pl.BlockSpec((nt, HEAD_DIM, tq), lambda b, i: (b, 0, 0)),
            pl.BlockSpec((None, LANES, tq), lambda b, i: (b * nt + i, 0, 0)),
        ],
        out_specs=pl.BlockSpec((None, tq, BRANCH_W), lambda b, i: (b, i, 0)),
        out_shape=jax.ShapeDtypeStruct((B, S, BRANCH_W), BF16),
        scratch_shapes=[pltpu.VMEM((nt + 1, tq, tq), F32)],
        compiler_params=_params("arbitrary", "arbitrary"),
        name="attn_select",
    )(z3, z3, z3, vt, sidet)


def _merge_kernel(oa_ref, ob_ref, oc_ref, od_ref, wb_ref, ga_ref, gb_ref, gc_ref, gd_ref, m_ref):
    acc = None
    for idx, (o_ref, g_ref) in enumerate(((oa_ref, ga_ref), (ob_ref, gb_ref), (oc_ref, gc_ref), (od_ref, gd_ref))):
        term = g_ref[...].astype(F32) * _dot(o_ref[...], wb_ref[idx])
        acc = term if acc is None else acc + term
    m_ref[...] = acc.astype(BF16)


def _merge(outs, wb, z, layer, tm):
    T = z.shape[0]
    D = wb.shape[-1]
    nn = D // TN
    o_spec = pl.BlockSpec((tm, BRANCH_W), lambda i, n: (i, 0))
    g_specs = [pl.BlockSpec((tm, TN), functools.partial(lambda i, n, br: (i, T_GATE + br * nn + n), br=br))
               for br in range(N_BRANCH)]
    return pl.pallas_call(
        _merge_kernel,
        grid=(T // tm, nn),
        in_specs=[o_spec] * 4 + [pl.BlockSpec((None, N_BRANCH, BRANCH_W, TN), lambda i, n: (layer, 0, 0, n))] + g_specs,
        out_specs=pl.BlockSpec((tm, TN), lambda i, n: (i, n)),
        out_shape=jax.ShapeDtypeStruct((T, D), BF16),
        compiler_params=_params("arbitrary", "arbitrary"),
        name="gated_merge",
    )(*outs, wb, z, z, z, z)


def _out_proj_kernel(m_ref, w_ref, x_ref, o_ref):
    o_ref[...] = x_ref[...] + _dot(m_ref[...], w_ref[...])


def _out_proj(merged, w, x2, layer, tm):
    T, D = x2.shape
    return pl.pallas_call(
        _out_proj_kernel,
        grid=(T // tm, D // TN),
        in_specs=[
            pl.BlockSpec((tm, D), lambda i, n: (i, 0)),
            pl.BlockSpec((None, D, TN), lambda i, n: (layer, 0, n)),
            pl.BlockSpec((tm, TN), lambda i, n: (i, n)),
        ],
        out_specs=pl.BlockSpec((tm, TN), lambda i, n: (i, n)),
        out_shape=jax.ShapeDtypeStruct((T, D), F32),
        compiler_params=_params("arbitrary", "arbitrary"),
        name="out_proj",
    )(merged, w, x2)


def _ffn_kernel(x_ref, g_ref, w1_ref, w2_ref, fg_ref, o_ref, h_ref, *, final_norm):
    f = pl.program_id(1)

    @pl.when(f == 0)
    def _():
        x = x_ref[...]
        y = x * lax.rsqrt(jnp.mean(x * x, axis=-1, keepdims=True) + EPS)
        h_ref[...] = (y * g_ref[...]).astype(BF16)
        o_ref[...] = x

    u = jnp.maximum(_dot(h_ref[...], w1_ref[...]), 0.0)
    o_ref[...] += _dot((u * u).astype(BF16), w2_ref[...])

    if final_norm:
        @pl.when(f == pl.num_programs(1) - 1)
        def _():
            y = o_ref[...]
            y = y * lax.rsqrt(jnp.mean(y * y, axis=-1, keepdims=True) + EPS)
            o_ref[...] = y * fg_ref[...]


def _ffn(x2, g, w1, w2, fg, layer, tm, tf, final_norm):
    T, D = x2.shape
    dff = w1.shape[-1]
    return pl.pallas_call(
        functools.partial(_ffn_kernel, final_norm=final_norm),
        grid=(T // tm, dff // tf),
        in_specs=[
            pl.BlockSpec((tm, D), lambda i, f: (i, 0)),
            pl.BlockSpec((1, D), lambda i, f: (0, 0)),
            pl.BlockSpec((None, D, tf), lambda i, f: (layer, 0, f)),
            pl.BlockSpec((None, tf, D), lambda i, f: (layer, f, 0)),
            pl.BlockSpec((1, D), lambda i, f: (0, 0)),
        ],
        out_specs=pl.BlockSpec((tm, D), lambda i, f: (i, 0)),
        out_shape=jax.ShapeDtypeStruct((T, D), F32),
        scratch_shapes=[pltpu.VMEM((tm, D), BF16)],
        compiler_params=_params("arbitrary", "arbitrary"),
        name="ffn",
    )(x2, g, w1, w2, fg)


def _w_in_offsets(n_in):
    off = {}
    o = 0
    for name, width in (("abc", 9 * BRANCH_W), ("c_f", HEADS), ("d_q", BRANCH_W), ("d_k", HEAD_DIM),
                        ("d_v", HEAD_DIM), ("d_iq", IDX_HEADS * IDX_DIM), ("d_ik", IDX_DIM),
                        ("d_iw", IDX_HEADS), ("gate", None)):
        width = n_in - o if width is None else width
        off[name] = (o, o + width)
        o += width
    return off


def _pack_w_in(w_in):
    L, D, n_in = w_in.shape
    off = _w_in_offsets(n_in)
    wt = jnp.transpose(w_in, (2, 0, 1)).astype(BF16)

    def seg(name):
        a, b = off[name]
        return wt[a:b]

    pad = jnp.zeros((LANES - HEADS - IDX_HEADS, L, D), BF16)
    packed = jnp.concatenate(
        [seg("abc"), seg("d_q"), seg("d_iq"),
         seg("d_k"), seg("d_v"), seg("d_ik"), seg("d_ik"), seg("c_f"), seg("d_iw"), pad,
         seg("gate")], axis=0)
    return packed.reshape(N_TILES * TN, L * D)


def _rope_table(seq):
    pos = jnp.arange(seq, dtype=F32)[:, None]

    def cs(dim):
        inv = ROPE_THETA ** (-jnp.arange(0, dim, 2, dtype=F32) / dim)
        ang = pos * inv[None, :]
        return jnp.cos(ang), jnp.sin(ang)

    c64, s64 = cs(DIFF_DIM)
    c128, s128 = cs(HEAD_DIM)
    return jnp.concatenate(
        [jnp.tile(c64, (1, 4)), jnp.tile(jnp.concatenate([-s64, s64], axis=1), (1, 2)),
         jnp.tile(c128, (1, 2)), jnp.concatenate([-s128, s128], axis=1)], axis=1)


def kernel(x, norm1_g, norm2_g, final_g, w_in, b_gate, b_forget, rel_bias, lambda_q1, lambda_k1,
           lambda_q2, lambda_k2, diff_norm_g, w_branch, w_out, w_ff1, w_ff2):
    B, S, D = x.shape
    depth = w_in.shape[0]
    T = B * S
    k_sel = min(TOPK_MAX, S // 4)
    tm = min(1024, S)
    tm_ffn = min(1024, S)
    tf = 512
    tq_a = 128
    tq = 256

    w_in_p = _pack_w_in(w_in)
    wb = w_branch.astype(BF16)
    wo = w_out.astype(BF16)
    w1 = w_ff1.astype(BF16)
    w2 = w_ff2.astype(BF16)
    rope = _rope_table(S)
    zeros_pre = jnp.zeros((depth, T_GATE * TN), F32)
    bias_in = jnp.concatenate([zeros_pre, b_gate.astype(F32)], axis=1)[:, None, :]
    bf = jnp.pad(b_forget.astype(F32), ((0, 0), (SIDE_F, LANES - SIDE_F - HEADS)))
    lamv = jnp.stack([lambda_q1, lambda_k1, lambda_q2, lambda_k2], axis=1).astype(F32)

    x2 = x.reshape(T, D)
    for l in range(depth):
        z, side, sidet, vtb, vtc, vtd = _in_proj(
            x2, norm1_g[l][None].astype(F32), w_in_p, bias_in, rope, bf[l][None], l, S, tm, tq)
        z3 = z.reshape(B, S, N_TILES * TN)
        ka, qa = _cum(side.reshape(B, S, LANES), tq)
        lambda_init = 0.8 - 0.6 * math.exp(-0.3 * l)
        oa = _attn_a(z3, _band_table(rel_bias[l], tq_a), tq_a)
        ob = _attn_b(z3, vtb, lamv[l], diff_norm_g[l][None].astype(F32), tq, lambda_init)
        oc = _attn_c(z3, vtc, ka, qa, tq)
        od = _attn_d(z3, vtd, sidet, tq, k_sel)
        outs = [o.reshape(T, BRANCH_W) for o in (oa, ob, oc, od)]
        merged = _merge(outs, wb, z, l, tm)
        x2 = _out_proj(merged, wo, x2, l, tm)
        x2 = _ffn(x2, norm2_g[l][None].astype(F32), w1, w2, final_g[None].astype(F32),
                  l, tm_ffn, tf, final_norm=(l == depth - 1))
    return x2.reshape(B, S, D)
```

```python
import functools
import math

import numpy as np
import jax
import jax.numpy as jnp
from jax import lax
from jax.experimental import pallas as pl
from jax.experimental.pallas import tpu as pltpu

F32 = jnp.float32
BF16 = jnp.bfloat16
I32 = jnp.int32

CHUNK = 64
N_BRANCH = 4
HEAD_DIM = 128
HEADS = 4
BRANCH_W = HEADS * HEAD_DIM
A_LEFT_CHUNKS = 8
A_LEFT = A_LEFT_CHUNKS * CHUNK
REL_CLIP = 128
DIFF_DIM = HEAD_DIM // 2
IDX_HEADS = 8
IDX_DIM = 64
TOPK_MAX = 256
ROPE_THETA = 10000.0
EPS = 1e-6

LANES = 128
LOG2E = 1.4426950408889634
NEG = -1e30
INT_MIN = -2147483648
VMEM_LIMIT = 56 * 1024 * 1024

TN = 512
T_AQ, T_AK, T_AV, T_BQ, T_BK, T_BV, T_CQ, T_CK, T_CV, T_DQ, T_DIQ, T_MISC, T_GATE = range(13)
N_TILES = T_GATE + N_BRANCH * 4
MISC_DK, MISC_DV, MISC_IK, MISC_SIDE = 0, 128, 256, 384
SIDE_F, SIDE_IW = 0, 4


def _dot(a, b):
    return jnp.dot(a, b, preferred_element_type=F32)


def _dot_t(a, b):
    return lax.dot_general(a, b, (((1,), (1,)), ((), ())), preferred_element_type=F32)


def _params(*sem):
    return pltpu.CompilerParams(dimension_semantics=sem, vmem_limit_bytes=VMEM_LIMIT)


def _head(h):
    return slice(h * HEAD_DIM, (h + 1) * HEAD_DIM)


def _rope_half(blk, cos, sin_signed, half):
    if 2 * half == LANES:
        rot = pltpu.roll(blk, half, axis=1)
    else:
        lane = lax.broadcasted_iota(I32, blk.shape, 1)
        first = (lane & (2 * half - 1)) < half
        rot = jnp.where(first, pltpu.roll(blk, LANES - half, axis=1), pltpu.roll(blk, half, axis=1))
    return blk * cos + rot * sin_signed


def _in_proj_kernel(x_ref, g_ref, w_ref, bias_ref, gen_ref, rope_ref, bf_ref,
                    z_ref, side_ref, sidet_ref, vtb_ref, vtc_ref, vtd_ref, h_ref, acc_ref, *, tk):
    j = pl.program_id(1)
    jp = j - 1
    n_sub = x_ref.shape[0] // tk

    def matmul():
        acc_ref[...] = _dot_t(h_ref[...], w_ref[...])

    def cols(a, c):
        return a[:, c * LANES:(c + 1) * LANES]

    def store_t(dst_ref, a):
        for c in range(n_sub):
            dst_ref[c] = a[c * tk:(c + 1) * tk, :].T.astype(dst_ref.dtype)

    def rotary_epilogue(prev):
        for c in range(4):
            blk = cols(prev, c)
            out = (blk * gen_ref[:, 0:128] + pltpu.roll(blk, 32, axis=1) * gen_ref[:, 128:256]
                   + pltpu.roll(blk, 96, axis=1) * gen_ref[:, 256:384]
                   + pltpu.roll(blk, 64, axis=1) * gen_ref[:, 384:512])
            z_ref[:, c * LANES:(c + 1) * LANES] = out.astype(BF16)

    def misc_epilogue(prev):
        z_ref[:, MISC_DK:MISC_DK + LANES] = _rope_half(
            cols(prev, 0), rope_ref[:, 256:384], rope_ref[:, 384:512], 64).astype(BF16)
        dv = cols(prev, 1)
        z_ref[:, MISC_DV:MISC_DV + LANES] = dv.astype(BF16)
        store_t(vtd_ref, dv)
        z_ref[:, MISC_IK:MISC_IK + LANES] = _rope_half(
            cols(prev, 2), rope_ref[:, 0:128], rope_ref[:, 128:256], 32).astype(BF16)
        raw = cols(prev, 3)
        z_ref[:, MISC_SIDE:MISC_SIDE + LANES] = raw.astype(BF16)
        lane = lax.broadcasted_iota(I32, raw.shape, 1)
        xf = raw + bf_ref[...]
        log_f = jnp.minimum(xf, 0.0) - jnp.log(1.0 + jnp.exp(-jnp.abs(xf)))
        iw = raw * (IDX_HEADS ** -0.5)
        side = jnp.where(lane < SIDE_IW, log_f, jnp.where(lane < SIDE_IW + IDX_HEADS, iw, 0.0))
        side_ref[...] = side
        store_t(sidet_ref, side)

    def gate_epilogue(prev):
        z_ref[...] = (0.5 * jnp.tanh(0.5 * (prev + bias_ref[...])) + 0.5).astype(BF16)

    @pl.when(j == 0)
    def _():
        x = x_ref[...]
        y = x * lax.rsqrt(jnp.mean(x * x, axis=-1, keepdims=True) + EPS)
        h_ref[...] = (y * g_ref[...]).astype(BF16)
        matmul()

    @pl.when((j >= 1) & (jp <= T_DIQ))
    def _():
        rotary_epilogue(acc_ref[...])
        matmul()

    @pl.when(jp == T_BV)
    def _():
        store_t(vtb_ref, z_ref[...].astype(F32))

    @pl.when(jp == T_CV)
    def _():
        store_t(vtc_ref, z_ref[...].astype(F32))

    @pl.when(jp == T_MISC)
    def _():
        misc_epilogue(acc_ref[...])
        matmul()

    @pl.when((jp >= T_GATE) & (j < N_TILES))
    def _():
        gate_epilogue(acc_ref[...])
        matmul()

    @pl.when(j == N_TILES)
    def _():
        gate_epilogue(acc_ref[...])


def _in_proj(x2, g, w, bias, gen, rope, bf, layer, seq, tm, tk):
    T, D = x2.shape
    n_pos = seq // tm
    n_sub = tm // tk

    def t_spec(rows):
        return pl.BlockSpec((n_sub, rows, tk), lambda i, j: (i, 0, 0))

    def prev_tile(j):
        return jnp.maximum(j - 1, 0)

    def kind(j):
        jp = j - 1
        return jnp.where(jp >= T_DIQ, 3, jnp.where(jp == T_DQ, 2, jnp.where((jp == T_BQ) | (jp == T_BK), 1, 0)))

    return pl.pallas_call(
        functools.partial(_in_proj_kernel, tk=tk),
        grid=(T // tm, N_TILES + 1),
        in_specs=[
            pl.BlockSpec((tm, D), lambda i, j: (i, 0)),
            pl.BlockSpec((1, D), lambda i, j: (0, 0)),
            pl.BlockSpec((TN, D), lambda i, j: (jnp.minimum(j, N_TILES - 1), layer)),
            pl.BlockSpec((None, 1, TN), lambda i, j: (layer, 0, prev_tile(j))),
            pl.BlockSpec((None, tm, TN), lambda i, j: (kind(j), i % n_pos, 0)),
            pl.BlockSpec((tm, TN), lambda i, j: (i % n_pos, 0)),
            pl.BlockSpec((1, LANES), lambda i, j: (0, 0)),
        ],
        out_specs=[
            pl.BlockSpec((tm, TN), lambda i, j: (i, prev_tile(j))),
            pl.BlockSpec((tm, LANES), lambda i, j: (i, 0)),
            t_spec(LANES), t_spec(BRANCH_W), t_spec(BRANCH_W), t_spec(HEAD_DIM),
        ],
        out_shape=[
            jax.ShapeDtypeStruct((T, N_TILES * TN), BF16),
            jax.ShapeDtypeStruct((T, LANES), F32),
            jax.ShapeDtypeStruct((T // tk, LANES, tk), F32),
            jax.ShapeDtypeStruct((T // tk, BRANCH_W, tk), BF16),
            jax.ShapeDtypeStruct((T // tk, BRANCH_W, tk), BF16),
            jax.ShapeDtypeStruct((T // tk, HEAD_DIM, tk), BF16),
        ],
        scratch_shapes=[pltpu.VMEM((tm, D), BF16), pltpu.VMEM((tm, TN), F32)],
        compiler_params=_params("arbitrary", "arbitrary"),
        name="in_proj",
    )(x2, g, w, bias, gen, rope, bf)


def _split3(x):
    hi = x.astype(BF16)
    r1 = x - hi.astype(F32)
    mid = r1.astype(BF16)
    lo = (r1 - mid.astype(F32)).astype(BF16)
    return hi, mid, lo


def _cum_kernel(side_ref, ka_ref, qa_ref, *, blk):
    S = side_ref.shape[0]
    r = lax.broadcasted_iota(I32, (blk, blk), 0)
    c = lax.broadcasted_iota(I32, (blk, blk), 1)
    tri = jnp.where(c <= r, 1.0, 0.0).astype(BF16)
    lane = lax.broadcasted_iota(I32, (blk, HEAD_DIM), 1)
    carry = jnp.zeros((1, LANES), F32)
    for b in range(S // blk):
        rows = slice(b * blk, (b + 1) * blk)
        hi, mid, lo = _split3(side_ref[rows, :])
        cb = (_dot(tri, hi) + _dot(tri, mid)) + _dot(tri, lo) + carry
        carry = cb[blk - 1:blk, :]
        for h in range(HEADS):
            parts = _split3(cb[:, SIDE_F + h:SIDE_F + h + 1] * (HEAD_DIM ** 0.5))
            ka = jnp.where((lane >= 3) & (lane < 6), 1.0, 0.0)
            qa = jnp.where(lane < 3, 1.0, 0.0)
            for n, part in enumerate(parts):
                ka = jnp.where(lane == n, -part.astype(F32), ka)
                qa = jnp.where(lane == 3 + n, part.astype(F32), qa)
            ka_ref[rows, _head(h)] = ka.astype(BF16)
            qa_ref[rows, _head(h)] = qa.astype(BF16)


def _cum(side3, blk):
    B, S, _ = side3.shape
    spec = pl.BlockSpec((None, S, BRANCH_W), lambda b: (b, 0, 0))
    shape = jax.ShapeDtypeStruct((B, S, BRANCH_W), BF16)
    return pl.pallas_call(
        functools.partial(_cum_kernel, blk=blk),
        grid=(B,),
        in_specs=[pl.BlockSpec((None, S, LANES), lambda b: (b, 0, 0))],
        out_specs=[spec, spec],
        out_shape=[shape, shape],
        compiler_params=_params("arbitrary"),
        name="forget_cumsum",
    )(side3)


def _online_multi(sts, vts, carries):
    stats = []
    for st, (m, l, acc) in zip(sts, carries):
        m_new = jnp.maximum(m, jnp.max(st, axis=0, keepdims=True))
        alpha = jnp.exp2(m - m_new)
        p = jnp.exp2(st - m_new)
        stats.append((m_new, alpha, alpha * l + jnp.sum(p, axis=0, keepdims=True), p.astype(BF16)))
    out = []
    for vt, (m_new, alpha, l, p), (_, _, acc) in zip(vts, stats, carries):
        out.append((m_new, l, alpha * acc + _dot(vt, p)))
    return out


def _flash_loop(i, scores, values, tail_mask, init):
    def pair(ta, carry, masked):
        sa, sb = scores(ta), scores(ta + 1)
        if masked:
            sa = [tail_mask(s, ta) for s in sa]
            sb = [tail_mask(s, ta + 1) for s in sb]
        carry = _online_multi(sa, values(ta), carry)
        return tuple(_online_multi(sb, values(ta + 1), carry))

    half = lax.shift_right_logical(i, 1)
    carry = lax.fori_loop(0, half, lambda p, c: pair(2 * p, c, False), tuple(init))
    return pair(2 * half, carry, True)


def _softmax_init_t(cols):
    return (jnp.full((1, cols), NEG, F32), jnp.zeros((1, cols), F32), jnp.zeros((HEAD_DIM, cols), F32))


def _attn_a_kernel(q_ref, k_ref, v_ref, bias_ref, o_ref, *, tq, win):
    i = pl.program_id(1)
    start = pl.multiple_of(jnp.maximum(i * tq - A_LEFT, 0), tq)
    scale = HEAD_DIM ** -0.5
    ss = [_dot_t(q_ref[:, _head(h)], k_ref[pl.ds(start, win), _head(h)]) * scale + bias_ref[h]
          for h in range(HEADS)]
    ps = []
    for s in ss:
        p = jnp.exp(s - jnp.max(s, axis=-1, keepdims=True))
        ps.append((p.astype(BF16), jnp.sum(p, axis=-1, keepdims=True)))
    for h, (p, l) in enumerate(ps):
        o_ref[:, _head(h)] = (_dot(p, v_ref[pl.ds(start, win), _head(h)]) / l).astype(BF16)


def _attn_a(z3, table, tq):
    B, S, _ = z3.shape
    nvar, _, _, win = table.shape
    return pl.pallas_call(
        functools.partial(_attn_a_kernel, tq=tq, win=win),
        grid=(B, S // tq),
        in_specs=[
            pl.BlockSpec((None, tq, TN), lambda b, i: (b, i, T_AQ)),
            pl.BlockSpec((None, S, TN), lambda b, i: (b, 0, T_AK)),
            pl.BlockSpec((None, S, TN), lambda b, i: (b, 0, T_AV)),
            pl.BlockSpec((None, HEADS, tq, win), lambda b, i: (jnp.minimum(i, nvar - 1), 0, 0, 0)),
        ],
        out_specs=pl.BlockSpec((None, tq, BRANCH_W), lambda b, i: (b, i, 0)),
        out_shape=jax.ShapeDtypeStruct((B, S, BRANCH_W), BF16),
        compiler_params=_params("arbitrary", "arbitrary"),
        name="attn_band",
    )(z3, z3, z3, table)


def _band_table(rel_bias_l, tq):
    H = rel_bias_l.shape[0]
    win = A_LEFT + tq
    nvar = A_LEFT // tq + 1
    wfull = win + A_LEFT
    r0 = A_LEFT + tq - 1
    length = wfull + tq - 1
    rb = rel_bias_l.astype(F32)
    g = jnp.concatenate(
        [jnp.broadcast_to(rb[:, 2 * REL_CLIP:], (H, r0 - REL_CLIP)),
         rb[:, ::-1],
         jnp.broadcast_to(rb[:, :1], (H, length - (r0 - REL_CLIP) - (2 * REL_CLIP + 1)))], axis=1)
    gp = jnp.pad(g, ((0, 0), (0, 1)))
    skew = jnp.tile(gp, (1, tq))[:, :tq * length].reshape(H, tq, length)
    full = skew[:, :, tq - 1:tq - 1 + wfull]

    qi = np.arange(tq)[:, None]
    kj = np.arange(win)[None, :]
    tables = []
    for v in range(nvar):
        d = A_LEFT - v * tq
        kk = kj + d
        cq = (A_LEFT + qi) // CHUNK
        ck = kk // CHUNK
        valid = (ck <= cq) & (ck >= cq - A_LEFT_CHUNKS)
        tables.append(jnp.where(valid[None], full[:, :, d:d + win], NEG))
    return jnp.stack(tables)


def _attn_b_kernel(q_ref, k_ref, vt_ref, lam_ref, g_ref, o_ref, *, tq, lambda_init):
    i = pl.program_id(1)
    scale = DIFF_DIM ** -0.5 * LOG2E
    lv = lam_ref[...]
    lam = (jnp.exp(jnp.sum(lv[0:1] * lv[1:2], axis=-1, keepdims=True))
           - jnp.exp(jnp.sum(lv[2:3] * lv[3:4], axis=-1, keepdims=True)) + lambda_init)
    lane = lax.broadcasted_iota(I32, (tq, HEAD_DIM), 1)
    qq = []
    for h in range(HEADS):
        q = q_ref[:, _head(h)]
        zero = jnp.zeros_like(q)
        qq.append(jnp.concatenate([jnp.where(lane < DIFF_DIM, q, zero), jnp.where(lane >= DIFF_DIM, q, zero)], axis=0))
    kchunk = lax.broadcasted_iota(I32, (tq, 2 * tq), 0) >> 6
    qchunk = (lax.broadcasted_iota(I32, (tq, 2 * tq), 1) & (tq - 1)) >> 6
    chunk_lead = kchunk - qchunk
    last = vt_ref.shape[0] - 1

    def scores(t):
        ks = pl.multiple_of(jnp.minimum(t, last) * tq, tq)
        return [_dot_t(k_ref[pl.ds(ks, tq), _head(h)], qq[h]) * scale for h in range(HEADS)]

    def values(t):
        return [vt_ref[jnp.minimum(t, last), _head(h), :] for h in range(HEADS)]

    def tail_mask(s, t):
        return jnp.where(chunk_lead <= (i - t) * (tq // CHUNK), s, NEG)

    carry = _flash_loop(i, scores, values, tail_mask, [_softmax_init_t(2 * tq) for _ in range(HEADS)])
    for h in range(HEADS):
        m, l, acc = carry[h]
        on = acc / l
        o = (on[:, :tq] - lam * on[:, tq:]).T
        y = o * lax.rsqrt(jnp.mean(o * o, axis=-1, keepdims=True) + EPS)
        o_ref[:, _head(h)] = ((y * g_ref[...]) * (1.0 - lambda_init)).astype(BF16)


def _attn_b(z3, vt, lamv, g, tq, lambda_init):
    B, S, _ = z3.shape
    nt = S // tq
    return pl.pallas_call(
        functools.partial(_attn_b_kernel, tq=tq, lambda_init=lambda_init),
        grid=(B, nt),
        in_specs=[
            pl.BlockSpec((None, tq, TN), lambda b, i: (b, i, T_BQ)),
            pl.BlockSpec((None, S, TN), lambda b, i: (b, 0, T_BK)),
            pl.BlockSpec((nt, BRANCH_W, tq), lambda b, i: (b, 0, 0)),
            pl.BlockSpec((4, DIFF_DIM), lambda b, i: (0, 0)),
            pl.BlockSpec((1, HEAD_DIM), lambda b, i: (0, 0)),
        ],
        out_specs=pl.BlockSpec((None, tq, BRANCH_W), lambda b, i: (b, i, 0)),
        out_shape=jax.ShapeDtypeStruct((B, S, BRANCH_W), BF16),
        compiler_params=_params("arbitrary", "arbitrary"),
        name="attn_diff",
    )(z3, z3, vt, lamv, g)


def _attn_c_kernel(q_ref, k_ref, vt_ref, ka_ref, qa_ref, o_ref, *, tq):
    i = pl.program_id(1)
    scale = HEAD_DIM ** -0.5 * LOG2E
    q_aug = [jnp.concatenate([q_ref[:, _head(h)], qa_ref[:, _head(h)]], axis=1) for h in range(HEADS)]
    key_lead = lax.broadcasted_iota(I32, (tq, tq), 0) - lax.broadcasted_iota(I32, (tq, tq), 1)
    last = vt_ref.shape[0] - 1

    def scores(t):
        ks = pl.multiple_of(jnp.minimum(t, last) * tq, tq)
        out = []
        for h in range(HEADS):
            k_aug = jnp.concatenate([k_ref[pl.ds(ks, tq), _head(h)], ka_ref[pl.ds(ks, tq), _head(h)]], axis=1)
            out.append(_dot_t(k_aug, q_aug[h]) * scale)
        return out

    def values(t):
        return [vt_ref[jnp.minimum(t, last), _head(h), :] for h in range(HEADS)]

    def tail_mask(s, t):
        return jnp.where(key_lead <= (i - t) * tq, s, NEG)

    carry = _flash_loop(i, scores, values, tail_mask, [_softmax_init_t(tq) for _ in range(HEADS)])
    for h in range(HEADS):
        m, l, acc = carry[h]
        o_ref[:, _head(h)] = (acc / l).T.astype(BF16)


def _attn_c(z3, vt, ka, qa, tq):
    B, S, _ = z3.shape
    nt = S // tq
    return pl.pallas_call(
        functools.partial(_attn_c_kernel, tq=tq),
        grid=(B, nt),
        in_specs=[
            pl.BlockSpec((None, tq, TN), lambda b, i: (b, i, T_CQ)),
            pl.BlockSpec((None, S, TN), lambda b, i: (b, 0, T_CK)),
            pl.BlockSpec((nt, BRANCH_W, tq), lambda b, i: (b, 0, 0)),
            pl.BlockSpec((None, S, BRANCH_W), lambda b, i: (b, 0, 0)),
            pl.BlockSpec((None, tq, BRANCH_W), lambda b, i: (b, i, 0)),
        ],
        out_specs=pl.BlockSpec((None, tq, BRANCH_W), lambda b, i: (b, i, 0)),
        out_shape=jax.ShapeDtypeStruct((B, S, BRANCH_W), BF16),
        compiler_params=_params("arbitrary", "arbitrary"),
        name="attn_forget",
    )(z3, z3, vt, ka, qa)


def _attn_d_kernel(q_ref, iq_ref, kv_ref, vt_ref, sidet_ref, o_ref, sc_ref, *, tq, k_sel):
    i = pl.program_id(1)
    n_t = i + 1
    scale = HEAD_DIM ** -0.5 * LOG2E
    ninf = float("-inf")
    qpos = i * tq + lax.broadcasted_iota(I32, (1, tq), 1)
    k_row = jnp.minimum(k_sel, ((qpos >> 6) + 1) * CHUNK).astype(F32)
    lane = lax.broadcasted_iota(I32, (tq, LANES), 1)
    ok = ((i * tq + lax.broadcasted_iota(I32, (tq, tq), 0)) >> 6) <= ((i * tq + lax.broadcasted_iota(I32, (tq, tq), 1)) >> 6)

    lhs, iws = [], []
    for hi in range(IDX_HEADS):
        blk = iq_ref[:, (hi // 2) * LANES:(hi // 2 + 1) * LANES]
        in_half = (lane >= IDX_DIM) if hi % 2 else (lane < IDX_DIM)
        lhs.append(jnp.where(in_half, blk, jnp.zeros_like(blk)))
        iws.append(sidet_ref[SIDE_IW + hi:SIDE_IW + hi + 1, :])

    def score_tile(t, masked):
        ks = pl.multiple_of(t * tq, tq)
        ik = kv_ref[pl.ds(ks, tq), MISC_IK:MISC_IK + LANES]
        sc = iws[0] * jnp.maximum(_dot_t(ik, lhs[0]), 0.0)
        for hi in range(1, IDX_HEADS):
            sc = sc + iws[hi] * jnp.maximum(_dot_t(ik, lhs[hi]), 0.0)
        if masked:
            sc = jnp.where(ok, sc, ninf)
        sc_ref[t] = sc

    def p1(t, c):
        score_tile(t, False)
        return c

    lax.fori_loop(0, i, p1, 0)
    score_tile(i, True)
    sc_ref[n_t] = jnp.full((tq, tq), ninf, F32)

    def count(pred_fn):
        def body(p, c):
            for t in (2 * p, 2 * p + 1):
                c = c + jnp.sum(jnp.where(pred_fn(sc_ref[t]), 1.0, 0.0), axis=0, keepdims=True)
            return c

        return lax.fori_loop(0, (n_t + 1) >> 1, body, jnp.zeros((1, tq), F32))

    def as_float(u):
        key = u ^ INT_MIN
        return lax.bitcast_convert_type(jnp.where(key < 0, key ^ 0x7FFFFFFF, key), F32)

    def bit_step(it, ans):
        cand = ans | lax.shift_left(jnp.int32(1), 31 - it)
        cf = as_float(cand)
        cnt = count(lambda st: st >= cf)
        return jnp.where(cnt >= k_row, cand, ans)

    thr = as_float(lax.fori_loop(0, 32, bit_step, jnp.zeros((1, tq), I32)))
    cnt_ge = count(lambda st: st >= thr)

    @pl.when(jnp.max(cnt_ge - k_row) > 0.0)
    def _():
        need = k_row - count(lambda st: st > thr)
        r = lax.broadcasted_iota(I32, (tq, tq), 0)
        c = lax.broadcasted_iota(I32, (tq, tq), 1)
        lower = jnp.where(c <= r, 1.0, 0.0).astype(BF16)

        def body(t, run):
            st = sc_ref[t]
            eq = st == thr
            rank = run + _dot(lower, jnp.where(eq, 1.0, 0.0).astype(BF16))
            sc_ref[t] = jnp.where(eq & (rank > need), ninf, st)
            return rank[tq - 1:tq, :]

        lax.fori_loop(0, n_t, body, jnp.zeros((1, tq), F32))

        key_idx = lax.broadcasted_iota(I32, (tq, tq), 0).astype(F32)

        def drop_round(e):
            def smallest(t, v):
                st = sc_ref[t]
                return jnp.minimum(v, jnp.min(jnp.where(st >= thr, st, jnp.inf), axis=0, keepdims=True))

            v = lax.fori_loop(0, n_t, smallest, jnp.full((1, tq), jnp.inf, F32))

            def last(t, j):
                idx = key_idx + (t * tq).astype(F32)
                return jnp.maximum(j, jnp.max(jnp.where(sc_ref[t] == v, idx, -1.0), axis=0, keepdims=True))

            j = lax.fori_loop(0, n_t, last, jnp.full((1, tq), -1.0, F32))

            def kill(t, c):
                idx = key_idx + (t * tq).astype(F32)
                sc_ref[t] = jnp.where((e > 0.0) & (idx == j), ninf, sc_ref[t])
                return c

            lax.fori_loop(0, n_t, kill, 0)
            return e - jnp.where(e > 0.0, 1.0, 0.0)

        lax.while_loop(lambda e: jnp.max(e) > 0.0, drop_round, count(lambda st: st >= thr) - k_row)

    q4 = jnp.concatenate([q_ref[:, _head(h)] for h in range(HEADS)], axis=0)

    last = vt_ref.shape[0] - 1

    def scores(t):
        ks = pl.multiple_of(jnp.minimum(t, last) * tq, tq)
        s = _dot_t(kv_ref[pl.ds(ks, tq), MISC_DK:MISC_DK + LANES], q4) * scale
        sel = sc_ref[t] >= thr
        return [jnp.concatenate([jnp.where(sel, s[:, h * tq:(h + 1) * tq], NEG) for h in range(HEADS)], axis=1)]

    (m, l, acc), = _flash_loop(i, scores, lambda t: [vt_ref[jnp.minimum(t, last)]], lambda s, t: s,
                               [_softmax_init_t(HEADS * tq)])
    o = acc / l
    for h in range(HEADS):
        o_ref[:, _head(h)] = o[:, h * tq:(h + 1) * tq].T.astype(BF16)


def _attn_d(z3, vt, sidet, tq, k_sel):
    B, S, _ = z3.shape
    nt = S // tq
    return pl.pallas_call(
        functools.partial(_attn_d_kernel, tq=tq, k_sel=k_sel),
        grid=(B, nt),
        in_specs=[
            pl.BlockSpec((None, tq, TN), lambda b, i: (b, i, T_DQ)),
            pl.BlockSpec((None, tq, TN), lambda b, i: (b, i, T_DIQ)),
            pl.BlockSpec((None, S, TN), lambda b, i: (b, 0, T_MISC)),
            pl.BlockSpec((nt, HEAD_DIM, tq), lambda b, i: (b, 0, 0)),
            pl.BlockSpec((None, LANES, tq), lambda b, i: (b * nt + i, 0, 0)),
        ],
        out_specs=pl.BlockSpec((None, tq, BRANCH_W), lambda b, i: (b, i, 0)),
        out_shape=jax.ShapeDtypeStruct((B, S, BRANCH_W), BF16),
        scratch_shapes=[pltpu.VMEM((nt + 1, tq, tq), F32)],
        compiler_params=_params("arbitrary", "arbitrary"),
        name="attn_select",
    )(z3, z3, z3, vt, sidet)


def _merge_kernel(oa_ref, ob_ref, oc_ref, od_ref, wb_ref, ga_ref, gb_ref, gc_ref, gd_ref, m_ref):
    acc = None
    for idx, (o_ref, g_ref) in enumerate(((oa_ref, ga_ref), (ob_ref, gb_ref), (oc_ref, gc_ref), (od_ref, gd_ref))):
        term = g_ref[...].astype(F32) * _dot(o_ref[...], wb_ref[idx])
        acc = term if acc is None else acc + term
    m_ref[...] = acc.astype(BF16)


def _merge(outs, wb, z, layer, tm):
    T = z.shape[0]
    D = wb.shape[-1]
    nn = D // TN
    o_spec = pl.BlockSpec((tm, BRANCH_W), lambda i, n: (i, 0))
    g_specs = [pl.BlockSpec((tm, TN), functools.partial(lambda i, n, br: (i, T_GATE + br * nn + n), br=br))
               for br in range(N_BRANCH)]
    return pl.pallas_call(
        _merge_kernel,
        grid=(T // tm, nn),
        in_specs=[o_spec] * 4 + [pl.BlockSpec((None, N_BRANCH, BRANCH_W, TN), lambda i, n: (layer, 0, 0, n))] + g_specs,
        out_specs=pl.BlockSpec((tm, TN), lambda i, n: (i, n)),
        out_shape=jax.ShapeDtypeStruct((T, D), BF16),
        compiler_params=_params("arbitrary", "arbitrary"),
        name="gated_merge",
    )(*outs, wb, z, z, z, z)


def _out_proj_kernel(m_ref, w_ref, x_ref, o_ref):
    o_ref[...] = x_ref[...] + _dot(m_ref[...], w_ref[...])


def _out_proj(merged, w, x2, layer, tm):
    T, D = x2.shape
    return pl.pallas_call(
        _out_proj_kernel,
        grid=(T // tm, D // TN),
        in_specs=[
            pl.BlockSpec((tm, D), lambda i, n: (i, 0)),
            pl.BlockSpec((None, D, TN), lambda i, n: (layer, 0, n)),
            pl.BlockSpec((tm, TN), lambda i, n: (i, n)),
        ],
        out_specs=pl.BlockSpec((tm, TN), lambda i, n: (i, n)),
        out_shape=jax.ShapeDtypeStruct((T, D), F32),
        compiler_params=_params("arbitrary", "arbitrary"),
        name="out_proj",
    )(merged, w, x2)


def _ffn_kernel(x_ref, g_ref, w1_ref, w2_ref, fg_ref, o_ref, h_ref, *, final_norm):
    f = pl.program_id(1)

    @pl.when(f == 0)
    def _():
        x = x_ref[...]
        y = x * lax.rsqrt(jnp.mean(x * x, axis=-1, keepdims=True) + EPS)
        h_ref[...] = (y * g_ref[...]).astype(BF16)
        o_ref[...] = x

    u = jnp.maximum(_dot(h_ref[...], w1_ref[...]), 0.0)
    o_ref[...] += _dot((u * u).astype(BF16), w2_ref[...])

    if final_norm:
        @pl.when(f == pl.num_programs(1) - 1)
        def _():
            y = o_ref[...]
            y = y * lax.rsqrt(jnp.mean(y * y, axis=-1, keepdims=True) + EPS)
            o_ref[...] = y * fg_ref[...]


def _ffn(x2, g, w1, w2, fg, layer, tm, tf, final_norm):
    T, D = x2.shape
    dff = w1.shape[-1]
    return pl.pallas_call(
        functools.partial(_ffn_kernel, final_norm=final_norm),
        grid=(T // tm, dff // tf),
        in_specs=[
            pl.BlockSpec((tm, D), lambda i, f: (i, 0)),
            pl.BlockSpec((1, D), lambda i, f: (0, 0)),
            pl.BlockSpec((None, D, tf), lambda i, f: (layer, 0, f)),
            pl.BlockSpec((None, tf, D), lambda i, f: (layer, f, 0)),
            pl.BlockSpec((1, D), lambda i, f: (0, 0)),
        ],
        out_specs=pl.BlockSpec((tm, D), lambda i, f: (i, 0)),
        out_shape=jax.ShapeDtypeStruct((T, D), F32),
        scratch_shapes=[pltpu.VMEM((tm, D), BF16)],
        compiler_params=_params("arbitrary", "arbitrary"),
        name="ffn",
    )(x2, g, w1, w2, fg)


def _w_in_offsets(n_in):
    off = {}
    o = 0
    for name, width in (("abc", 9 * BRANCH_W), ("c_f", HEADS), ("d_q", BRANCH_W), ("d_k", HEAD_DIM),
                        ("d_v", HEAD_DIM), ("d_iq", IDX_HEADS * IDX_DIM), ("d_ik", IDX_DIM),
                        ("d_iw", IDX_HEADS), ("gate", None)):
        width = n_in - o if width is None else width
        off[name] = (o, o + width)
        o += width
    return off


def _pack_w_in(w_in):
    L, D, n_in = w_in.shape
    off = _w_in_offsets(n_in)
    wt = jnp.transpose(w_in, (2, 0, 1)).astype(BF16)

    def seg(name):
        a, b = off[name]
        return wt[a:b]

    pad = jnp.zeros((LANES - HEADS - IDX_HEADS, L, D), BF16)
    packed = jnp.concatenate(
        [seg("abc"), seg("d_q"), seg("d_iq"),
         seg("d_k"), seg("d_v"), seg("d_ik"), seg("d_ik"), seg("c_f"), seg("d_iw"), pad,
         seg("gate")], axis=0)
    return packed.reshape(N_TILES * TN, L * D)


def _rope_table(seq):
    pos = jnp.arange(seq, dtype=F32)[:, None]

    def cs(dim):
        inv = ROPE_THETA ** (-jnp.arange(0, dim, 2, dtype=F32) / dim)
        ang = pos * inv[None, :]
        return jnp.cos(ang), jnp.sin(ang)

    c64, s64 = cs(DIFF_DIM)
    c128, s128 = cs(HEAD_DIM)
    return jnp.concatenate(
        [jnp.tile(c64, (1, 4)), jnp.tile(jnp.concatenate([-s64, s64], axis=1), (1, 2)),
         jnp.tile(c128, (1, 2)), jnp.concatenate([-s128, s128], axis=1)], axis=1)


def _gen_table(rope):
    seq = rope.shape[0]
    c64, s64, c128, s128 = (rope[:, k * LANES:(k + 1) * LANES] for k in range(4))
    first = (np.arange(LANES)[None, :] % DIFF_DIM) < DIFF_DIM // 2
    zero = jnp.zeros((seq, LANES), F32)
    one = jnp.ones((seq, LANES), F32)
    r64 = jnp.concatenate([c64, jnp.where(first, 0.0, s64), jnp.where(first, s64, 0.0), zero], axis=1)
    return jnp.stack([
        jnp.concatenate([one, zero, zero, zero], axis=1),
        r64,
        jnp.concatenate([c128, zero, zero, s128], axis=1),
        r64 * (IDX_DIM ** -0.5),
    ])


def kernel(x, norm1_g, norm2_g, final_g, w_in, b_gate, b_forget, rel_bias, lambda_q1, lambda_k1,
           lambda_q2, lambda_k2, diff_norm_g, w_branch, w_out, w_ff1, w_ff2):
    B, S, D = x.shape
    depth = w_in.shape[0]
    T = B * S
    k_sel = min(TOPK_MAX, S // 4)
    tm = min(1024, S)
    tm_ffn = min(1024, S)
    tf = 512
    tq_a = 128
    tq = 256

    w_in_p = _pack_w_in(w_in)
    wb = w_branch.astype(BF16)
    wo = w_out.astype(BF16)
    w1 = w_ff1.astype(BF16)
    w2 = w_ff2.astype(BF16)
    rope = _rope_table(S)
    gen = _gen_table(rope)
    zeros_pre = jnp.zeros((depth, T_GATE * TN), F32)
    bias_in = jnp.concatenate([zeros_pre, b_gate.astype(F32)], axis=1)[:, None, :]
    bf = jnp.pad(b_forget.astype(F32), ((0, 0), (SIDE_F, LANES - SIDE_F - HEADS)))
    lamv = jnp.stack([lambda_q1, lambda_k1, lambda_q2, lambda_k2], axis=1).astype(F32)

    x2 = x.reshape(T, D)
    for l in range(depth):
        z, side, sidet, vtb, vtc, vtd = _in_proj(
            x2, norm1_g[l][None].astype(F32), w_in_p, bias_in, gen, rope, bf[l][None], l, S, tm, tq)
        z3 = z.reshape(B, S, N_TILES * TN)
        ka, qa = _cum(side.reshape(B, S, LANES), tq)
        lambda_init = 0.8 - 0.6 * math.exp(-0.3 * l)
        oa = _attn_a(z3, _band_table(rel_bias[l], tq_a), tq_a)
        ob = _attn_b(z3, vtb, lamv[l], diff_norm_g[l][None].astype(F32), tq, lambda_init)
        oc = _attn_c(z3, vtc, ka, qa, tq)
        od = _attn_d(z3, vtd, sidet, tq, k_sel)
        outs = [o.reshape(T, BRANCH_W) for o in (oa, ob, oc, od)]
        merged = _merge(outs, wb, z, l, tm)
        x2 = _out_proj(merged, wo, x2, l, tm)
        x2 = _ffn(x2, norm2_g[l][None].astype(F32), w1, w2, final_g[None].astype(F32),
                  l, tm_ffn, tf, final_norm=(l == depth - 1))
    return x2.reshape(B, S, D)
```

```python
import functools
import math

import numpy as np
import jax
import jax.numpy as jnp
from jax import lax
from jax.experimental import pallas as pl
from jax.experimental.pallas import tpu as pltpu

F32 = jnp.float32
BF16 = jnp.bfloat16
I32 = jnp.int32

CHUNK = 64
N_BRANCH = 4
HEAD_DIM = 128
HEADS = 4
BRANCH_W = HEADS * HEAD_DIM
A_LEFT_CHUNKS = 8
A_LEFT = A_LEFT_CHUNKS * CHUNK
REL_CLIP = 128
DIFF_DIM = HEAD_DIM // 2
IDX_HEADS = 8
IDX_DIM = 64
TOPK_MAX = 256
ROPE_THETA = 10000.0
EPS = 1e-6

LANES = 128
LOG2E = 1.4426950408889634
NEG = -1e30
INT_MIN = -2147483648
VMEM_LIMIT = 56 * 1024 * 1024

TN = 512
T_AQ, T_AK, T_AV, T_BQ, T_BK, T_BV, T_CQ, T_CK, T_CV, T_DQ, T_DIQ, T_MISC, T_GATE = range(13)
N_TILES = T_GATE + N_BRANCH * 4
MISC_DK, MISC_DV, MISC_IK, MISC_SIDE = 0, 128, 256, 384
SIDE_F, SIDE_IW = 0, 4


def _dot(a, b):
    return jnp.dot(a, b, preferred_element_type=F32)


def _dot_t(a, b):
    return lax.dot_general(a, b, (((1,), (1,)), ((), ())), preferred_element_type=F32)


def _params(*sem):
    return pltpu.CompilerParams(dimension_semantics=sem, vmem_limit_bytes=VMEM_LIMIT)


def _head(h):
    return slice(h * HEAD_DIM, (h + 1) * HEAD_DIM)


def _rope_half(blk, cos, sin_signed, half):
    if 2 * half == LANES:
        rot = pltpu.roll(blk, half, axis=1)
    else:
        lane = lax.broadcasted_iota(I32, blk.shape, 1)
        first = (lane & (2 * half - 1)) < half
        rot = jnp.where(first, pltpu.roll(blk, LANES - half, axis=1), pltpu.roll(blk, half, axis=1))
    return blk * cos + rot * sin_signed


def _in_proj_kernel(x_ref, g_ref, w_ref, bias_ref, gen_ref, rope_ref, bf_ref,
                    z_ref, side_ref, sidet_ref, vtb_ref, vtc_ref, vtd_ref, h_ref, acc_ref, *, tk):
    j = pl.program_id(1)
    jp = j - 1
    n_sub = x_ref.shape[0] // tk

    def matmul():
        acc_ref[...] = _dot_t(h_ref[...], w_ref[...])

    def cols(a, c):
        return a[:, c * LANES:(c + 1) * LANES]

    def store_t(dst_ref, a):
        for c in range(n_sub):
            dst_ref[c] = a[c * tk:(c + 1) * tk, :].T.astype(dst_ref.dtype)

    def rotary_epilogue(prev):
        for c in range(4):
            blk = cols(prev, c)
            out = (blk * gen_ref[:, 0:128] + pltpu.roll(blk, 32, axis=1) * gen_ref[:, 128:256]
                   + pltpu.roll(blk, 96, axis=1) * gen_ref[:, 256:384]
                   + pltpu.roll(blk, 64, axis=1) * gen_ref[:, 384:512])
            z_ref[:, c * LANES:(c + 1) * LANES] = out.astype(BF16)

    def misc_epilogue(prev):
        z_ref[:, MISC_DK:MISC_DK + LANES] = _rope_half(
            cols(prev, 0), rope_ref[:, 256:384], rope_ref[:, 384:512], 64).astype(BF16)
        dv = cols(prev, 1)
        z_ref[:, MISC_DV:MISC_DV + LANES] = dv.astype(BF16)
        store_t(vtd_ref, dv)
        z_ref[:, MISC_IK:MISC_IK + LANES] = _rope_half(
            cols(prev, 2), rope_ref[:, 0:128], rope_ref[:, 128:256], 32).astype(BF16)
        raw = cols(prev, 3)
        z_ref[:, MISC_SIDE:MISC_SIDE + LANES] = raw.astype(BF16)
        lane = lax.broadcasted_iota(I32, raw.shape, 1)
        xf = raw + bf_ref[...]
        log_f = jnp.minimum(xf, 0.0) - jnp.log(1.0 + jnp.exp(-jnp.abs(xf)))
        iw = raw * (IDX_HEADS ** -0.5)
        side = jnp.where(lane < SIDE_IW, log_f, jnp.where(lane < SIDE_IW + IDX_HEADS, iw, 0.0))
        side_ref[...] = side
        store_t(sidet_ref, side)

    def gate_epilogue(prev):
        z_ref[...] = (0.5 * jnp.tanh(0.5 * (prev + bias_ref[...])) + 0.5).astype(BF16)

    @pl.when(j == 0)
    def _():
        x = x_ref[...]
        y = x * lax.rsqrt(jnp.mean(x * x, axis=-1, keepdims=True) + EPS)
        h_ref[...] = (y * g_ref[...]).astype(BF16)
        matmul()

    @pl.when((j >= 1) & (jp <= T_DIQ))
    def _():
        rotary_epilogue(acc_ref[...])
        matmul()

    @pl.when(jp == T_BV)
    def _():
        store_t(vtb_ref, z_ref[...].astype(F32))

    @pl.when(jp == T_CV)
    def _():
        store_t(vtc_ref, z_ref[...].astype(F32))

    @pl.when(jp == T_MISC)
    def _():
        misc_epilogue(acc_ref[...])
        matmul()

    @pl.when((jp >= T_GATE) & (j < N_TILES))
    def _():
        gate_epilogue(acc_ref[...])
        matmul()

    @pl.when(j == N_TILES)
    def _():
        gate_epilogue(acc_ref[...])


def _in_proj(x2, g, w, bias, gen, rope, bf, layer, seq, tm, tk):
    T, D = x2.shape
    n_pos = seq // tm
    n_sub = tm // tk

    def t_spec(rows):
        return pl.BlockSpec((n_sub, rows, tk), lambda i, j: (i, 0, 0))

    def prev_tile(j):
        return jnp.maximum(j - 1, 0)

    def kind(j):
        jp = j - 1
        return jnp.where(jp >= T_DIQ, 3, jnp.where(jp == T_DQ, 2, jnp.where((jp == T_BQ) | (jp == T_BK), 1, 0)))

    return pl.pallas_call(
        functools.partial(_in_proj_kernel, tk=tk),
        grid=(T // tm, N_TILES + 1),
        in_specs=[
            pl.BlockSpec((tm, D), lambda i, j: (i, 0)),
            pl.BlockSpec((1, D), lambda i, j: (0, 0)),
            pl.BlockSpec((TN, D), lambda i, j: (jnp.minimum(j, N_TILES - 1), layer)),
            pl.BlockSpec((None, 1, TN), lambda i, j: (layer, 0, prev_tile(j))),
            pl.BlockSpec((None, tm, TN), lambda i, j: (kind(j), i % n_pos, 0)),
            pl.BlockSpec((tm, TN), lambda i, j: (i % n_pos, 0)),
            pl.BlockSpec((1, LANES), lambda i, j: (0, 0)),
        ],
        out_specs=[
            pl.BlockSpec((tm, TN), lambda i, j: (i, prev_tile(j))),
            pl.BlockSpec((tm, LANES), lambda i, j: (i, 0)),
            t_spec(LANES), t_spec(BRANCH_W), t_spec(BRANCH_W), t_spec(HEAD_DIM),
        ],
        out_shape=[
            jax.ShapeDtypeStruct((T, N_TILES * TN), BF16),
            jax.ShapeDtypeStruct((T, LANES), F32),
            jax.ShapeDtypeStruct((T // tk, LANES, tk), F32),
            jax.ShapeDtypeStruct((T // tk, BRANCH_W, tk), BF16),
            jax.ShapeDtypeStruct((T // tk, BRANCH_W, tk), BF16),
            jax.ShapeDtypeStruct((T // tk, HEAD_DIM, tk), BF16),
        ],
        scratch_shapes=[pltpu.VMEM((tm, D), BF16), pltpu.VMEM((tm, TN), F32)],
        compiler_params=_params("arbitrary", "arbitrary"),
        name="in_proj",
    )(x2, g, w, bias, gen, rope, bf)


def _split3(x):
    hi = x.astype(BF16)
    r1 = x - hi.astype(F32)
    mid = r1.astype(BF16)
    lo = (r1 - mid.astype(F32)).astype(BF16)
    return hi, mid, lo


def _cum_kernel(side_ref, ka_ref, qa_ref, *, blk):
    S = side_ref.shape[0]
    r = lax.broadcasted_iota(I32, (blk, blk), 0)
    c = lax.broadcasted_iota(I32, (blk, blk), 1)
    tri = jnp.where(c <= r, 1.0, 0.0).astype(BF16)
    lane = lax.broadcasted_iota(I32, (blk, HEAD_DIM), 1)
    carry = jnp.zeros((1, LANES), F32)
    for b in range(S // blk):
        rows = slice(b * blk, (b + 1) * blk)
        hi, mid, lo = _split3(side_ref[rows, :])
        cb = (_dot(tri, hi) + _dot(tri, mid)) + _dot(tri, lo) + carry
        carry = cb[blk - 1:blk, :]
        for h in range(HEADS):
            parts = _split3(cb[:, SIDE_F + h:SIDE_F + h + 1] * (HEAD_DIM ** 0.5))
            ka = jnp.where((lane >= 3) & (lane < 6), 1.0, 0.0)
            qa = jnp.where(lane < 3, 1.0, 0.0)
            for n, part in enumerate(parts):
                ka = jnp.where(lane == n, -part.astype(F32), ka)
                qa = jnp.where(lane == 3 + n, part.astype(F32), qa)
            ka_ref[rows, _head(h)] = ka.astype(BF16)
            qa_ref[rows, _head(h)] = qa.astype(BF16)


def _cum(side3, blk):
    B, S, _ = side3.shape
    spec = pl.BlockSpec((None, S, BRANCH_W), lambda b: (b, 0, 0))
    shape = jax.ShapeDtypeStruct((B, S, BRANCH_W), BF16)
    return pl.pallas_call(
        functools.partial(_cum_kernel, blk=blk),
        grid=(B,),
        in_specs=[pl.BlockSpec((None, S, LANES), lambda b: (b, 0, 0))],
        out_specs=[spec, spec],
        out_shape=[shape, shape],
        compiler_params=_params("arbitrary"),
        name="forget_cumsum",
    )(side3)


def _online_multi(sts, vts, carries):
    stats = []
    for st, (m, l, acc) in zip(sts, carries):
        m_new = jnp.maximum(m, jnp.max(st, axis=0, keepdims=True))
        alpha = jnp.exp2(m - m_new)
        p = jnp.exp2(st - m_new)
        stats.append((m_new, alpha, alpha * l + jnp.sum(p, axis=0, keepdims=True), p.astype(BF16)))
    out = []
    for vt, (m_new, alpha, l, p), (_, _, acc) in zip(vts, stats, carries):
        out.append((m_new, l, alpha * acc + _dot(vt, p)))
    return out


def _flash_loop(i, scores, values, tail_mask, init):
    def pair(ta, carry, masked):
        sa, sb = scores(ta), scores(ta + 1)
        if masked:
            sa = [tail_mask(s, ta) for s in sa]
            sb = [tail_mask(s, ta + 1) for s in sb]
        carry = _online_multi(sa, values(ta), carry)
        return tuple(_online_multi(sb, values(ta + 1), carry))

    half = lax.shift_right_logical(i, 1)
    carry = lax.fori_loop(0, half, lambda p, c: pair(2 * p, c, False), tuple(init))
    return pair(2 * half, carry, True)


def _softmax_init_t(cols):
    return (jnp.full((1, cols), NEG, F32), jnp.zeros((1, cols), F32), jnp.zeros((HEAD_DIM, cols), F32))


def _attn_a_kernel(q_ref, k_ref, v_ref, bias_ref, o_ref, *, tq, win):
    i = pl.program_id(1)
    start = pl.multiple_of(jnp.maximum(i * tq - A_LEFT, 0), tq)
    scale = HEAD_DIM ** -0.5
    ss = [_dot_t(q_ref[:, _head(h)], k_ref[pl.ds(start, win), _head(h)]) * scale + bias_ref[h]
          for h in range(HEADS)]
    ps = []
    for s in ss:
        p = jnp.exp(s - jnp.max(s, axis=-1, keepdims=True))
        ps.append((p.astype(BF16), jnp.sum(p, axis=-1, keepdims=True)))
    for h, (p, l) in enumerate(ps):
        o_ref[:, _head(h)] = (_dot(p, v_ref[pl.ds(start, win), _head(h)]) / l).astype(BF16)


def _attn_a(z3, table, tq):
    B, S, _ = z3.shape
    nvar, _, _, win = table.shape
    return pl.pallas_call(
        functools.partial(_attn_a_kernel, tq=tq, win=win),
        grid=(B, S // tq),
        in_specs=[
            pl.BlockSpec((None, tq, TN), lambda b, i: (b, i, T_AQ)),
            pl.BlockSpec((None, S, TN), lambda b, i: (b, 0, T_AK)),
            pl.BlockSpec((None, S, TN), lambda b, i: (b, 0, T_AV)),
            pl.BlockSpec((None, HEADS, tq, win), lambda b, i: (jnp.minimum(i, nvar - 1), 0, 0, 0)),
        ],
        out_specs=pl.BlockSpec((None, tq, BRANCH_W), lambda b, i: (b, i, 0)),
        out_shape=jax.ShapeDtypeStruct((B, S, BRANCH_W), BF16),
        compiler_params=_params("arbitrary", "arbitrary"),
        name="attn_band",
    )(z3, z3, z3, table)


def _band_table(rel_bias_l, tq):
    H = rel_bias_l.shape[0]
    win = A_LEFT + tq
    nvar = A_LEFT // tq + 1
    wfull = win + A_LEFT
    r0 = A_LEFT + tq - 1
    length = wfull + tq - 1
    rb = rel_bias_l.astype(F32)
    g = jnp.concatenate(
        [jnp.broadcast_to(rb[:, 2 * REL_CLIP:], (H, r0 - REL_CLIP)),
         rb[:, ::-1],
         jnp.broadcast_to(rb[:, :1], (H, length - (r0 - REL_CLIP) - (2 * REL_CLIP + 1)))], axis=1)
    gp = jnp.pad(g, ((0, 0), (0, 1)))
    skew = jnp.tile(gp, (1, tq))[:, :tq * length].reshape(H, tq, length)
    full = skew[:, :, tq - 1:tq - 1 + wfull]

    qi = np.arange(tq)[:, None]
    kj = np.arange(win)[None, :]
    tables = []
    for v in range(nvar):
        d = A_LEFT - v * tq
        kk = kj + d
        cq = (A_LEFT + qi) // CHUNK
        ck = kk // CHUNK
        valid = (ck <= cq) & (ck >= cq - A_LEFT_CHUNKS)
        tables.append(jnp.where(valid[None], full[:, :, d:d + win], NEG))
    return jnp.stack(tables)


def _attn_b_kernel(q_ref, k_ref, vt_ref, lam_ref, g_ref, o_ref, *, tq, lambda_init):
    i = pl.program_id(1)
    scale = DIFF_DIM ** -0.5 * LOG2E
    lv = lam_ref[...]
    lam = (jnp.exp(jnp.sum(lv[0:1] * lv[1:2], axis=-1, keepdims=True))
           - jnp.exp(jnp.sum(lv[2:3] * lv[3:4], axis=-1, keepdims=True)) + lambda_init)
    lane = lax.broadcasted_iota(I32, (tq, HEAD_DIM), 1)
    qq = []
    for h in range(HEADS):
        q = q_ref[:, _head(h)]
        zero = jnp.zeros_like(q)
        qq.append(jnp.concatenate([jnp.where(lane < DIFF_DIM, q, zero), jnp.where(lane >= DIFF_DIM, q, zero)], axis=0))
    kchunk = lax.broadcasted_iota(I32, (tq, 2 * tq), 0) >> 6
    qchunk = (lax.broadcasted_iota(I32, (tq, 2 * tq), 1) & (tq - 1)) >> 6
    chunk_lead = kchunk - qchunk
    last = vt_ref.shape[0] - 1

    def scores(t):
        ks = pl.multiple_of(jnp.minimum(t, last) * tq, tq)
        return [_dot_t(k_ref[pl.ds(ks, tq), _head(h)], qq[h]) * scale for h in range(HEADS)]

    def values(t):
        return [vt_ref[jnp.minimum(t, last), _head(h), :] for h in range(HEADS)]

    def tail_mask(s, t):
        return jnp.where(chunk_lead <= (i - t) * (tq // CHUNK), s, NEG)

    carry = _flash_loop(i, scores, values, tail_mask, [_softmax_init_t(2 * tq) for _ in range(HEADS)])
    for h in range(HEADS):
        m, l, acc = carry[h]
        on = acc / l
        o = (on[:, :tq] - lam * on[:, tq:]).T
        y = o * lax.rsqrt(jnp.mean(o * o, axis=-1, keepdims=True) + EPS)
        o_ref[:, _head(h)] = ((y * g_ref[...]) * (1.0 - lambda_init)).astype(BF16)


def _attn_b(z3, vt, lamv, g, tq, lambda_init):
    B, S, _ = z3.shape
    nt = S // tq
    return pl.pallas_call(
        functools.partial(_attn_b_kernel, tq=tq, lambda_init=lambda_init),
        grid=(B, nt),
        in_specs=[
            pl.BlockSpec((None, tq, TN), lambda b, i: (b, i, T_BQ)),
            pl.BlockSpec((None, S, TN), lambda b, i: (b, 0, T_BK)),
            pl.BlockSpec((nt, BRANCH_W, tq), lambda b, i: (b, 0, 0)),
            pl.BlockSpec((4, DIFF_DIM), lambda b, i: (0, 0)),
            pl.BlockSpec((1, HEAD_DIM), lambda b, i: (0, 0)),
        ],
        out_specs=pl.BlockSpec((None, tq, BRANCH_W), lambda b, i: (b, i, 0)),
        out_shape=jax.ShapeDtypeStruct((B, S, BRANCH_W), BF16),
        compiler_params=_params("arbitrary", "arbitrary"),
        name="attn_diff",
    )(z3, z3, vt, lamv, g)


def _attn_c_kernel(q_ref, k_ref, vt_ref, ka_ref, qa_ref, o_ref, *, tq):
    i = pl.program_id(1)
    scale = HEAD_DIM ** -0.5 * LOG2E
    q_aug = [jnp.concatenate([q_ref[:, _head(h)], qa_ref[:, _head(h)]], axis=1) for h in range(HEADS)]
    key_lead = lax.broadcasted_iota(I32, (tq, tq), 0) - lax.broadcasted_iota(I32, (tq, tq), 1)
    last = vt_ref.shape[0] - 1

    def scores(t):
        ks = pl.multiple_of(jnp.minimum(t, last) * tq, tq)
        out = []
        for h in range(HEADS):
            k_aug = jnp.concatenate([k_ref[pl.ds(ks, tq), _head(h)], ka_ref[pl.ds(ks, tq), _head(h)]], axis=1)
            out.append(_dot_t(k_aug, q_aug[h]) * scale)
        return out

    def values(t):
        return [vt_ref[jnp.minimum(t, last), _head(h), :] for h in range(HEADS)]

    def tail_mask(s, t):
        return jnp.where(key_lead <= (i - t) * tq, s, NEG)

    carry = _flash_loop(i, scores, values, tail_mask, [_softmax_init_t(tq) for _ in range(HEADS)])
    for h in range(HEADS):
        m, l, acc = carry[h]
        o_ref[:, _head(h)] = (acc / l).T.astype(BF16)


def _attn_c(z3, vt, ka, qa, tq):
    B, S, _ = z3.shape
    nt = S // tq
    return pl.pallas_call(
        functools.partial(_attn_c_kernel, tq=tq),
        grid=(B, nt),
        in_specs=[
            pl.BlockSpec((None, tq, TN), lambda b, i: (b, i, T_CQ)),
            pl.BlockSpec((None, S, TN), lambda b, i: (b, 0, T_CK)),
            pl.BlockSpec((nt, BRANCH_W, tq), lambda b, i: (b, 0, 0)),
            pl.BlockSpec((None, S, BRANCH_W), lambda b, i: (b, 0, 0)),
            pl.BlockSpec((None, tq, BRANCH_W), lambda b, i: (b, i, 0)),
        ],
        out_specs=pl.BlockSpec((None, tq, BRANCH_W), lambda b, i: (b, i, 0)),
        out_shape=jax.ShapeDtypeStruct((B, S, BRANCH_W), BF16),
        compiler_params=_params("arbitrary", "arbitrary"),
        name="attn_forget",
    )(z3, z3, vt, ka, qa)


def _attn_d_kernel(q_ref, iq_ref, kv_ref, vt_ref, sidet_ref, o_ref, sc_ref, *, tq, k_sel):
    i = pl.program_id(1)
    n_t = i + 1
    scale = HEAD_DIM ** -0.5 * LOG2E
    ninf = float("-inf")
    qpos = i * tq + lax.broadcasted_iota(I32, (1, tq), 1)
    k_row = jnp.minimum(k_sel, ((qpos >> 6) + 1) * CHUNK).astype(F32)
    lane = lax.broadcasted_iota(I32, (tq, LANES), 1)
    chunk_lead = (lax.broadcasted_iota(I32, (tq, tq), 0) >> 6) - (lax.broadcasted_iota(I32, (tq, tq), 1) >> 6)

    lhs, iws = [], []
    for hi in range(IDX_HEADS):
        blk = iq_ref[:, (hi // 2) * LANES:(hi // 2 + 1) * LANES]
        in_half = (lane >= IDX_DIM) if hi % 2 else (lane < IDX_DIM)
        lhs.append(jnp.where(in_half, blk, jnp.zeros_like(blk)))
        iws.append(sidet_ref[SIDE_IW + hi:SIDE_IW + hi + 1, :])

    last = vt_ref.shape[0] - 1

    def score_pair(ta, masked):
        tiles = (ta, ta + 1)
        dots = []
        for t in tiles:
            ks = pl.multiple_of(jnp.minimum(t, last) * tq, tq)
            ik = kv_ref[pl.ds(ks, tq), MISC_IK:MISC_IK + LANES]
            dots.append([_dot_t(ik, lhs[hi]) for hi in range(IDX_HEADS)])
        for t, d in zip(tiles, dots):
            sc = iws[0] * jnp.maximum(d[0], 0.0)
            for hi in range(1, IDX_HEADS):
                sc = sc + iws[hi] * jnp.maximum(d[hi], 0.0)
            if masked:
                sc = jnp.where(chunk_lead <= (i - t) * (tq // CHUNK), sc, ninf)
            sc_ref[t] = sc

    def p1(p, c):
        score_pair(2 * p, False)
        return c

    half = lax.shift_right_logical(i, 1)
    lax.fori_loop(0, half, p1, 0)
    score_pair(2 * half, True)
    sc_ref[n_t] = jnp.full((tq, tq), ninf, F32)

    def count(pred_fn):
        def body(p, c):
            for t in (2 * p, 2 * p + 1):
                hit = jnp.where(pred_fn(sc_ref[t]), 1.0, 0.0)
                c = c + jnp.sum(hit.reshape(tq // 8, 8, tq), axis=0)
            return c

        part = lax.fori_loop(0, (n_t + 1) >> 1, body, jnp.zeros((8, tq), F32))
        return jnp.sum(part, axis=0, keepdims=True)

    def as_float(u):
        key = u ^ INT_MIN
        return lax.bitcast_convert_type(jnp.where(key < 0, key ^ 0x7FFFFFFF, key), F32)

    def bit_step(it, ans):
        cand = ans | lax.shift_left(jnp.int32(1), 31 - it)
        cf = as_float(cand)
        cnt = count(lambda st: st >= cf)
        return jnp.where(cnt >= k_row, cand, ans)

    thr = as_float(lax.fori_loop(0, 32, bit_step, jnp.zeros((1, tq), I32)))
    cnt_ge = count(lambda st: st >= thr)

    @pl.when(jnp.max(cnt_ge - k_row) > 0.0)
    def _():
        need = k_row - count(lambda st: st > thr)
        r = lax.broadcasted_iota(I32, (tq, tq), 0)
        c = lax.broadcasted_iota(I32, (tq, tq), 1)
        lower = jnp.where(c <= r, 1.0, 0.0).astype(BF16)

        def body(t, run):
            st = sc_ref[t]
            eq = st == thr
            rank = run + _dot(lower, jnp.where(eq, 1.0, 0.0).astype(BF16))
            sc_ref[t] = jnp.where(eq & (rank > need), ninf, st)
            return rank[tq - 1:tq, :]

        lax.fori_loop(0, n_t, body, jnp.zeros((1, tq), F32))

        key_idx = lax.broadcasted_iota(I32, (tq, tq), 0).astype(F32)

        def drop_round(e):
            def smallest(t, v):
                st = sc_ref[t]
                return jnp.minimum(v, jnp.min(jnp.where(st >= thr, st, jnp.inf), axis=0, keepdims=True))

            v = lax.fori_loop(0, n_t, smallest, jnp.full((1, tq), jnp.inf, F32))

            def last(t, j):
                idx = key_idx + (t * tq).astype(F32)
                return jnp.maximum(j, jnp.max(jnp.where(sc_ref[t] == v, idx, -1.0), axis=0, keepdims=True))

            j = lax.fori_loop(0, n_t, last, jnp.full((1, tq), -1.0, F32))

            def kill(t, c):
                idx = key_idx + (t * tq).astype(F32)
                sc_ref[t] = jnp.where((e > 0.0) & (idx == j), ninf, sc_ref[t])
                return c

            lax.fori_loop(0, n_t, kill, 0)
            return e - jnp.where(e > 0.0, 1.0, 0.0)

        lax.while_loop(lambda e: jnp.max(e) > 0.0, drop_round, count(lambda st: st >= thr) - k_row)

    q4 = jnp.concatenate([q_ref[:, _head(h)] for h in range(HEADS)], axis=0)

    last = vt_ref.shape[0] - 1

    def scores(t):
        ks = pl.multiple_of(jnp.minimum(t, last) * tq, tq)
        s = _dot_t(kv_ref[pl.ds(ks, tq), MISC_DK:MISC_DK + LANES], q4) * scale
        sel = sc_ref[t] >= thr
        return [jnp.concatenate([jnp.where(sel, s[:, h * tq:(h + 1) * tq], NEG) for h in range(HEADS)], axis=1)]

    (m, l, acc), = _flash_loop(i, scores, lambda t: [vt_ref[jnp.minimum(t, last)]], lambda s, t: s,
                               [_softmax_init_t(HEADS * tq)])
    o = acc / l
    for h in range(HEADS):
        o_ref[:, _head(h)] = o[:, h * tq:(h + 1) * tq].T.astype(BF16)


def _attn_d(z3, vt, sidet, tq, k_sel):
    B, S, _ = z3.shape
    nt = S // tq
    return pl.pallas_call(
        functools.partial(_attn_d_kernel, tq=tq, k_sel=k_sel),
        grid=(B, nt),
        in_specs=[
            pl.BlockSpec((None, tq, TN), lambda b, i: (b, i, T_DQ)),
            pl.BlockSpec((None, tq, TN), lambda b, i: (b, i, T_DIQ)),
            pl.BlockSpec((None, S, TN), lambda b, i: (b, 0, T_MISC)),
            pl.BlockSpec((nt, HEAD_DIM, tq), lambda b, i: (b, 0, 0)),
            pl.BlockSpec((None, LANES, tq), lambda b, i: (b * nt + i, 0, 0)),
        ],
        out_specs=pl.BlockSpec((None, tq, BRANCH_W), lambda b, i: (b, i, 0)),
        out_shape=jax.ShapeDtypeStruct((B, S, BRANCH_W), BF16),
        scratch_shapes=[pltpu.VMEM((nt + 1, tq, tq), F32)],
        compiler_params=_params("arbitrary", "arbitrary"),
        name="attn_select",
    )(z3, z3, z3, vt, sidet)


def _merge_kernel(oa_ref, ob_ref, oc_ref, od_ref, wb_ref, ga_ref, gb_ref, gc_ref, gd_ref, m_ref):
    acc = None
    for idx, (o_ref, g_ref) in enumerate(((oa_ref, ga_ref), (ob_ref, gb_ref), (oc_ref, gc_ref), (od_ref, gd_ref))):
        term = g_ref[...].astype(F32) * _dot(o_ref[...], wb_ref[idx])
        acc = term if acc is None else acc + term
    m_ref[...] = acc.astype(BF16)


def _merge(outs, wb, z, layer, tm):
    T = z.shape[0]
    D = wb.shape[-1]
    nn = D // TN
    o_spec = pl.BlockSpec((tm, BRANCH_W), lambda i, n: (i, 0))
    g_specs = [pl.BlockSpec((tm, TN), functools.partial(lambda i, n, br: (i, T_GATE + br * nn + n), br=br))
               for br in range(N_BRANCH)]
    return pl.pallas_call(
        _merge_kernel,
        grid=(T // tm, nn),
        in_specs=[o_spec] * 4 + [pl.BlockSpec((None, N_BRANCH, BRANCH_W, TN), lambda i, n: (layer, 0, 0, n))] + g_specs,
        out_specs=pl.BlockSpec((tm, TN), lambda i, n: (i, n)),
        out_shape=jax.ShapeDtypeStruct((T, D), BF16),
        compiler_params=_params("arbitrary", "arbitrary"),
        name="gated_merge",
    )(*outs, wb, z, z, z, z)


def _out_proj_kernel(m_ref, w_ref, x_ref, o_ref):
    o_ref[...] = x_ref[...] + _dot(m_ref[...], w_ref[...])


def _out_proj(merged, w, x2, layer, tm):
    T, D = x2.shape
    return pl.pallas_call(
        _out_proj_kernel,
        grid=(T // tm, D // TN),
        in_specs=[
            pl.BlockSpec((tm, D), lambda i, n: (i, 0)),
            pl.BlockSpec((None, D, TN), lambda i, n: (layer, 0, n)),
            pl.BlockSpec((tm, TN), lambda i, n: (i, n)),
        ],
        out_specs=pl.BlockSpec((tm, TN), lambda i, n: (i, n)),
        out_shape=jax.ShapeDtypeStruct((T, D), F32),
        compiler_params=_params("arbitrary", "arbitrary"),
        name="out_proj",
    )(merged, w, x2)


def _ffn_kernel(x_ref, g_ref, w1_ref, w2_ref, fg_ref, o_ref, h_ref, *, final_norm):
    f = pl.program_id(1)

    @pl.when(f == 0)
    def _():
        x = x_ref[...]
        y = x * lax.rsqrt(jnp.mean(x * x, axis=-1, keepdims=True) + EPS)
        h_ref[...] = (y * g_ref[...]).astype(BF16)
        o_ref[...] = x

    u = jnp.maximum(_dot(h_ref[...], w1_ref[...]), 0.0)
    o_ref[...] += _dot((u * u).astype(BF16), w2_ref[...])

    if final_norm:
        @pl.when(f == pl.num_programs(1) - 1)
        def _():
            y = o_ref[...]
            y = y * lax.rsqrt(jnp.mean(y * y, axis=-1, keepdims=True) + EPS)
            o_ref[...] = y * fg_ref[...]


def _ffn(x2, g, w1, w2, fg, layer, tm, tf, final_norm):
    T, D = x2.shape
    dff = w1.shape[-1]
    return pl.pallas_call(
        functools.partial(_ffn_kernel, final_norm=final_norm),
        grid=(T // tm, dff // tf),
        in_specs=[
            pl.BlockSpec((tm, D), lambda i, f: (i, 0)),
            pl.BlockSpec((1, D), lambda i, f: (0, 0)),
            pl.BlockSpec((None, D, tf), lambda i, f: (layer, 0, f)),
            pl.BlockSpec((None, tf, D), lambda i, f: (layer, f, 0)),
            pl.BlockSpec((1, D), lambda i, f: (0, 0)),
        ],
        out_specs=pl.BlockSpec((tm, D), lambda i, f: (i, 0)),
        out_shape=jax.ShapeDtypeStruct((T, D), F32),
        scratch_shapes=[pltpu.VMEM((tm, D), BF16)],
        compiler_params=_params("arbitrary", "arbitrary"),
        name="ffn",
    )(x2, g, w1, w2, fg)


def _w_in_offsets(n_in):
    off = {}
    o = 0
    for name, width in (("abc", 9 * BRANCH_W), ("c_f", HEADS), ("d_q", BRANCH_W), ("d_k", HEAD_DIM),
                        ("d_v", HEAD_DIM), ("d_iq", IDX_HEADS * IDX_DIM), ("d_ik", IDX_DIM),
                        ("d_iw", IDX_HEADS), ("gate", None)):
        width = n_in - o if width is None else width
        off[name] = (o, o + width)
        o += width
    return off


def _pack_w_in(w_in):
    L, D, n_in = w_in.shape
    off = _w_in_offsets(n_in)
    wt = jnp.transpose(w_in, (2, 0, 1)).astype(BF16)

    def seg(name):
        a, b = off[name]
        return wt[a:b]

    pad = jnp.zeros((LANES - HEADS - IDX_HEADS, L, D), BF16)
    packed = jnp.concatenate(
        [seg("abc"), seg("d_q"), seg("d_iq"),
         seg("d_k"), seg("d_v"), seg("d_ik"), seg("d_ik"), seg("c_f"), seg("d_iw"), pad,
         seg("gate")], axis=0)
    return packed.reshape(N_TILES * TN, L * D)


def _rope_table(seq):
    pos = jnp.arange(seq, dtype=F32)[:, None]

    def cs(dim):
        inv = ROPE_THETA ** (-jnp.arange(0, dim, 2, dtype=F32) / dim)
        ang = pos * inv[None, :]
        return jnp.cos(ang), jnp.sin(ang)

    c64, s64 = cs(DIFF_DIM)
    c128, s128 = cs(HEAD_DIM)
    return jnp.concatenate(
        [jnp.tile(c64, (1, 4)), jnp.tile(jnp.concatenate([-s64, s64], axis=1), (1, 2)),
         jnp.tile(c128, (1, 2)), jnp.concatenate([-s128, s128], axis=1)], axis=1)


def _gen_table(rope):
    seq = rope.shape[0]
    c64, s64, c128, s128 = (rope[:, k * LANES:(k + 1) * LANES] for k in range(4))
    first = (np.arange(LANES)[None, :] % DIFF_DIM) < DIFF_DIM // 2
    zero = jnp.zeros((seq, LANES), F32)
    one = jnp.ones((seq, LANES), F32)
    r64 = jnp.concatenate([c64, jnp.where(first, 0.0, s64), jnp.where(first, s64, 0.0), zero], axis=1)
    return jnp.stack([
        jnp.concatenate([one, zero, zero, zero], axis=1),
        r64,
        jnp.concatenate([c128, zero, zero, s128], axis=1),
        r64 * (IDX_DIM ** -0.5),
    ])


def kernel(x, norm1_g, norm2_g, final_g, w_in, b_gate, b_forget, rel_bias, lambda_q1, lambda_k1,
           lambda_q2, lambda_k2, diff_norm_g, w_branch, w_out, w_ff1, w_ff2):
    B, S, D = x.shape
    depth = w_in.shape[0]
    T = B * S
    k_sel = min(TOPK_MAX, S // 4)
    tm = min(1024, S)
    tm_ffn = min(1024, S)
    tf = 512
    tq_a = 128
    tq = 256

    w_in_p = _pack_w_in(w_in)
    wb = w_branch.astype(BF16)
    wo = w_out.astype(BF16)
    w1 = w_ff1.astype(BF16)
    w2 = w_ff2.astype(BF16)
    rope = _rope_table(S)
    gen = _gen_table(rope)
    zeros_pre = jnp.zeros((depth, T_GATE * TN), F32)
    bias_in = jnp.concatenate([zeros_pre, b_gate.astype(F32)], axis=1)[:, None, :]
    bf = jnp.pad(b_forget.astype(F32), ((0, 0), (SIDE_F, LANES - SIDE_F - HEADS)))
    lamv = jnp.stack([lambda_q1, lambda_k1, lambda_q2, lambda_k2], axis=1).astype(F32)

    x2 = x.reshape(T, D)
    for l in range(depth):
        z, side, sidet, vtb, vtc, vtd = _in_proj(
            x2, norm1_g[l][None].astype(F32), w_in_p, bias_in, gen, rope, bf[l][None], l, S, tm, tq)
        z3 = z.reshape(B, S, N_TILES * TN)
        ka, qa = _cum(side.reshape(B, S, LANES), tq)
        lambda_init = 0.8 - 0.6 * math.exp(-0.3 * l)
        oa = _attn_a(z3, _band_table(rel_bias[l], tq_a), tq_a)
        ob = _attn_b(z3, vtb, lamv[l], diff_norm_g[l][None].astype(F32), tq, lambda_init)
        oc = _attn_c(z3, vtc, ka, qa, tq)
        od = _attn_d(z3, vtd, sidet, tq, k_sel)
        outs = [o.reshape(T, BRANCH_W) for o in (oa, ob, oc, od)]
        merged = _merge(outs, wb, z, l, tm)
        x2 = _out_proj(merged, wo, x2, l, tm)
        x2 = _ffn(x2, norm2_g[l][None].astype(F32), w1, w2, final_g[None].astype(F32),
                  l, tm_ffn, tf, final_norm=(l == depth - 1))
    return x2.reshape(B, S, D)
```
